```python
import math
import jax, jax.numpy as jnp
from jax import lax
import numpy as np

D_MODEL = 1024
BATCH = 8
SEQ = 2048
DEPTH = 4
DEC_BATCH = 16
DEC_SEQ = 64
PAST_LEN = 4096

CHUNK = 64
DIFF_HEADS = 8
DIFF_HD = 64
DSA_HEADS = 8
DSA_HD = 128
IDX_HEADS = 4
IDX_HD = 64
TOPK_MAX = 256
D_FF = 2816
CONV_W = 3
ROPE_THETA = 10000.0
EPS = 1e-6
Q_BLOCK_DENSE = 128
Q_BLOCK_SPARSE = 64

DIFF_WIDTH = DIFF_HEADS * 2 * DIFF_HD
DSA_WIDTH = DSA_HEADS * DSA_HD
IDX_Q_WIDTH = IDX_HEADS * IDX_HD
SPLIT_SIZES = (DIFF_WIDTH, DIFF_WIDTH, DIFF_WIDTH, DSA_WIDTH, DSA_WIDTH, DSA_WIDTH, IDX_Q_WIDTH, IDX_HD, IDX_HEADS)
N_IN = DIFF_WIDTH * 3 + DSA_WIDTH * 3 + IDX_Q_WIDTH + IDX_HD + IDX_HEADS

kernel_name = 'chunk_causal_diffattn_dsa_convffn_hybrid_step'


def rmsnorm(x, g):
    xf = x.astype(jnp.float32)
    y = xf * lax.rsqrt(jnp.mean(xf * xf, axis=-1, keepdims=True) + EPS)
    return (y * g.astype(jnp.float32)).astype(x.dtype)


def rope(x, pos):
    half = x.shape[-1] // 2
    inv = ROPE_THETA ** (-(jnp.arange(half, dtype=jnp.float32) / half))
    ang = pos.astype(jnp.float32)[:, None] * inv[None, :]
    cos = jnp.cos(ang)[None, :, None, :]
    sin = jnp.sin(ang)[None, :, None, :]
    xf = x.astype(jnp.float32)
    x1, x2 = xf[..., :half], xf[..., half:]
    return jnp.concatenate([x1 * cos - x2 * sin, x2 * cos + x1 * sin], axis=-1).astype(x.dtype)


def chunk_mask(q_pos, k_pos):
    return (k_pos[None, :] // CHUNK) <= (q_pos[:, None] // CHUNK)


def to_blocks(a, nb, qb):
    return jnp.moveaxis(a.reshape(a.shape[0], nb, qb, *a.shape[2:]), 1, 0)


def from_blocks(a):
    a = jnp.moveaxis(a, 0, 1)
    return a.reshape(a.shape[0], a.shape[1] * a.shape[2], *a.shape[3:])


def diff_attention(q, k, v, q_pos, k_pos, lam, subln_g, lam_init):
    B, nq = q.shape[0], q.shape[1]
    qb = math.gcd(nq, Q_BLOCK_DENSE)
    nb = nq // qb
    scale = DIFF_HD ** -0.5

    def block(args):
        qblk, pblk = args
        s = jnp.einsum('bqhd,bkhd->bhqk', qblk, k).astype(jnp.float32) * scale
        s = jnp.where(chunk_mask(pblk, k_pos)[None, None], s, -jnp.inf)
        p = jax.nn.softmax(s, axis=-1).reshape(B, DIFF_HEADS, 2, qb, -1)
        a = p[:, :, 0] - lam * p[:, :, 1]
        return jnp.einsum('bhqk,bkhe->bqhe', a.astype(v.dtype), v)

    o = from_blocks(lax.map(block, (to_blocks(q, nb, qb), q_pos.reshape(nb, qb))))
    o = rmsnorm(o, subln_g) * (1.0 - lam_init)
    return o.reshape(B, nq, DIFF_WIDTH)


def dsa_attention(q, qi, wi, q_pos, k, v, ki, k_pos):
    B, nq = q.shape[0], q.shape[1]
    L = k.shape[1]
    topk = min(TOPK_MAX, L // 4)
    qb = math.gcd(nq, Q_BLOCK_SPARSE)
    nb = nq // qb
    gather_rows = jax.vmap(lambda rows, idx: rows[idx])

    def block(args):
        qblk, qiblk, wiblk, pblk = args
        dots = jnp.einsum('bqhd,bsd->bqhs', qiblk, ki).astype(jnp.float32) * (IDX_HD ** -0.5)
        score = jnp.einsum('bqh,bqhs->bqs', wiblk.astype(jnp.float32), jax.nn.relu(dots))
        score = jnp.where(chunk_mask(pblk, k_pos)[None], score, -jnp.inf)
        vals, idx = lax.top_k(score, topk)
        valid = jnp.isfinite(vals)
        ks = gather_rows(k, idx)
        vs = gather_rows(v, idx)
        s = jnp.einsum('bqhd,bqkhd->bhqk', qblk, ks).astype(jnp.float32) * (DSA_HD ** -0.5)
        s = jnp.where(valid[:, None], s, -jnp.inf)
        p = jax.nn.softmax(s, axis=-1)
        return jnp.einsum('bhqk,bqkhd->bqhd', p.astype(vs.dtype), vs)

    o = from_blocks(lax.map(block, (to_blocks(q, nb, qb), to_blocks(qi, nb, qb), to_blocks(wi, nb, qb), q_pos.reshape(nb, qb))))
    return o.reshape(B, nq, DSA_WIDTH)


def split_cols(z):
    parts, start = [], 0
    for size in SPLIT_SIZES:
        parts.append(z[..., start:start + size])
        start += size
    return parts


def layer(x, past, past_len, l, norm_attn, w_in, diff_lambda, diff_subln, w_branch_a, w_branch_b,
          w_gate, b_gate, w_out, norm_ffn, w_up, conv_w, conv_b, w_down):
    B, T = x.shape[0], x.shape[1]
    q_pos = past_len + jnp.arange(T, dtype=jnp.int32)
    k_pos = jnp.arange(past_len + T, dtype=jnp.int32)

    h = rmsnorm(x, norm_attn)
    dq, dk, dv, sq, sk, sv, iq, ik, iw = split_cols(h @ w_in)
    dq = rope(dq.reshape(B, T, 2 * DIFF_HEADS, DIFF_HD), q_pos)
    dk = rope(dk.reshape(B, T, 2 * DIFF_HEADS, DIFF_HD), q_pos)
    dv = dv.reshape(B, T, DIFF_HEADS, 2 * DIFF_HD)
    sq = rope(sq.reshape(B, T, DSA_HEADS, DSA_HD), q_pos)
    sk = rope(sk.reshape(B, T, DSA_HEADS, DSA_HD), q_pos)
    sv = sv.reshape(B, T, DSA_HEADS, DSA_HD)
    iq = rope(iq.reshape(B, T, IDX_HEADS, IDX_HD), q_pos)
    ik = rope(ik.reshape(B, T, 1, IDX_HD), q_pos)[:, :, 0]
    iw = iw * (IDX_HEADS ** -0.5)

    if past is None:
        dk_all, dv_all, sk_all, sv_all, ik_all = dk, dv, sk, sv, ik
        conv_prev = jnp.zeros((B, CONV_W - 1, 2 * D_FF), x.dtype)
    else:
        c_dk, c_dv, c_sk, c_sv, c_ik, conv_prev = past
        dk_all = jnp.concatenate([c_dk.astype(dk.dtype), dk], axis=1)
        dv_all = jnp.concatenate([c_dv.astype(dv.dtype), dv], axis=1)
        sk_all = jnp.concatenate([c_sk.astype(sk.dtype), sk], axis=1)
        sv_all = jnp.concatenate([c_sv.astype(sv.dtype), sv], axis=1)
        ik_all = jnp.concatenate([c_ik.astype(ik.dtype), ik], axis=1)

    lam_init = 0.8 - 0.6 * math.exp(-0.3 * l)
    lf = diff_lambda.astype(jnp.float32)
    lam = jnp.exp(jnp.sum(lf[0] * lf[1])) - jnp.exp(jnp.sum(lf[2] * lf[3])) + lam_init

    o_a = diff_attention(dq, dk_all, dv_all, q_pos, k_pos, lam, diff_subln, lam_init) @ w_branch_a
    o_b = dsa_attention(sq, iq, iw, q_pos, sk_all, sv_all, ik_all, k_pos) @ w_branch_b
    gates = jax.nn.sigmoid(h @ w_gate + b_gate)
    merged = gates[..., :D_MODEL] * o_a + gates[..., D_MODEL:] * o_b
    x = x + merged @ w_out

    h2 = rmsnorm(x, norm_ffn)
    u = h2 @ w_up
    up = jnp.concatenate([conv_prev.astype(u.dtype), u], axis=1)
    c = conv_b
    for j in range(CONV_W):
        c = c + up[:, j:j + T] * conv_w[j]
    new_conv = up[:, T:]
    x = x + (jax.nn.silu(c[..., D_FF:]) * c[..., :D_FF]) @ w_down
    return x, (dk, dv, sk, sv, ik, new_conv)


def setup_inputs(seed: int = 0) -> dict:
    key = jax.random.key(seed)
    ks = jax.random.split(key, 24)
    f32 = jnp.float32

    def nrm(k, shape, scale):
        return jax.random.normal(k, shape, f32) * scale

    return {
        'x_prompt': nrm(ks[0], (BATCH, SEQ, D_MODEL), 1.0),
        'x_sample': nrm(ks[1], (DEC_BATCH, DEC_SEQ, D_MODEL), 1.0),
        'cache_diff_k': nrm(ks[2], (DEPTH, DEC_BATCH, PAST_LEN, 2 * DIFF_HEADS, DIFF_HD), 1.0),
        'cache_diff_v': nrm(ks[3], (DEPTH, DEC_BATCH, PAST_LEN, DIFF_HEADS, 2 * DIFF_HD), 1.0),
        'cache_dsa_k': nrm(ks[4], (DEPTH, DEC_BATCH, PAST_LEN, DSA_HEADS, DSA_HD), 1.0),
        'cache_dsa_v': nrm(ks[5], (DEPTH, DEC_BATCH, PAST_LEN, DSA_HEADS, DSA_HD), 1.0),
        'cache_idx_k': nrm(ks[6], (DEPTH, DEC_BATCH, PAST_LEN, IDX_HD), 1.0),
        'state_ffn_conv': nrm(ks[7], (DEPTH, DEC_BATCH, CONV_W - 1, 2 * D_FF), 1.0),
        'norm_attn': 1.0 + nrm(ks[8], (DEPTH, D_MODEL), 0.02),
        'w_in': nrm(ks[9], (DEPTH, D_MODEL, N_IN), D_MODEL ** -0.5),
        'diff_lambda': nrm(ks[10], (DEPTH, 4, DIFF_HD), 0.1),
        'diff_subln': 1.0 + nrm(ks[11], (DEPTH, 2 * DIFF_HD), 0.02),
        'w_branch_a': nrm(ks[12], (DEPTH, DIFF_WIDTH, D_MODEL), DIFF_WIDTH ** -0.5),
        'w_branch_b': nrm(ks[13], (DEPTH, DSA_WIDTH, D_MODEL), DSA_WIDTH ** -0.5),
        'w_gate': nrm(ks[14], (DEPTH, D_MODEL, 2 * D_MODEL), D_MODEL ** -0.5),
        'b_gate': nrm(ks[15], (DEPTH, 2 * D_MODEL), 0.01),
        'w_out': nrm(ks[16], (DEPTH, D_MODEL, D_MODEL), D_MODEL ** -0.5),
        'norm_ffn': 1.0 + nrm(ks[17], (DEPTH, D_MODEL), 0.02),
        'w_up': nrm(ks[18], (DEPTH, D_MODEL, 2 * D_FF), D_MODEL ** -0.5),
        'conv_w': nrm(ks[19], (DEPTH, CONV_W, 2 * D_FF), CONV_W ** -0.5),
        'conv_b': nrm(ks[20], (DEPTH, 2 * D_FF), 0.01),
        'w_down': nrm(ks[21], (DEPTH, D_FF, D_MODEL), D_FF ** -0.5),
        'norm_final': 1.0 + nrm(ks[22], (D_MODEL,), 0.02),
    }


def reference(x_prompt, x_sample, cache_diff_k, cache_diff_v, cache_dsa_k, cache_dsa_v, cache_idx_k,
              state_ffn_conv, norm_attn, w_in, diff_lambda, diff_subln, w_branch_a, w_branch_b, w_gate,
              b_gate, w_out, norm_ffn, w_up, conv_w, conv_b, w_down, norm_final):
    past_len = cache_diff_k.shape[2]
    xp, xs = x_prompt, x_sample
    rows_p, rows_s = [], []
    for l in range(DEPTH):
        weights = (norm_attn[l], w_in[l], diff_lambda[l], diff_subln[l], w_branch_a[l], w_branch_b[l],
                   w_gate[l], b_gate[l], w_out[l], norm_ffn[l], w_up[l], conv_w[l], conv_b[l], w_down[l])
        xp, new_p = layer(xp, None, 0, l, *weights)
        past = (cache_diff_k[l], cache_diff_v[l], cache_dsa_k[l], cache_dsa_v[l], cache_idx_k[l], state_ffn_conv[l])
        xs, new_s = layer(xs, past, past_len, l, *weights)
        rows_p.append(new_p)
        rows_s.append(new_s)
    y_prompt = rmsnorm(xp, norm_final)
    y_sample = rmsnorm(xs, norm_final)
    p_diff_k, p_diff_v, p_dsa_k, p_dsa_v, p_idx_k, p_ffn_conv = [jnp.stack(z, axis=0) for z in zip(*rows_p)]
    s_diff_k, s_diff_v, s_dsa_k, s_dsa_v, s_idx_k, s_ffn_conv = [jnp.stack(z, axis=0) for z in zip(*rows_s)]
    return (y_prompt, y_sample, p_diff_k, p_diff_v, p_dsa_k, p_dsa_v, p_idx_k, p_ffn_conv,
            s_diff_k, s_diff_v, s_dsa_k, s_dsa_v, s_idx_k, s_ffn_conv)
```

```python
import functools
import math

import jax
import jax.numpy as jnp
from jax import lax
from jax.experimental import pallas as pl
from jax.experimental.pallas import tpu as pltpu

F32 = jnp.float32
BF16 = jnp.bfloat16
I32 = jnp.int32

CHUNK = 64
CHUNK_SHIFT = 6
IDX_HEADS = 4
TOPK_MAX = 256
CONV_W = 3
ROPE_THETA = 10000.0
EPS = 1e-6

LANES = 128
NEG = -1e30
INT_MIN = -(2 ** 31)
KEY_NEG_INF = INT_MIN + 0x7FFFFF
KEY_POS_INF = 0x7F800000
VMEM_LIMIT = 56 * 1024 * 1024


def _cparams(n_axes):
    return pltpu.CompilerParams(dimension_semantics=("arbitrary",) * n_axes,
                                vmem_limit_bytes=VMEM_LIMIT)


def _rms(x, g):
    return x * lax.rsqrt(jnp.mean(x * x, axis=-1, keepdims=True) + EPS) * g


def _dot(a, b):
    return jnp.dot(a, b, preferred_element_type=F32)


def _dot_nt(a, b):
    return lax.dot_general(a, b, (((1,), (1,)), ((), ())), preferred_element_type=F32)


def _proj_kernel(x_ref, g_ref, w64_ref, w128_ref, wv_ref, wikw_ref,
                 c64_ref, sa64_ref, sb64_ref, c128_ref, s128_ref,
                 dq_ref, dkb_ref, dkf_ref, iq_ref, sq_ref, skb_ref, skf_ref,
                 dvb_ref, dvf_ref, svb_ref, svf_ref, ikb_ref, ikf_ref, iw_ref,
                 *, dw, sw, iqw, idx_hd):
    h = _rms(x_ref[...], g_ref[...]).astype(BF16)
    c64, sa64, sb64 = c64_ref[...], sa64_ref[...], sb64_ref[...]
    c128, s128 = c128_ref[...], s128_ref[...]

    def rope64(y):
        return y * c64 + pltpu.roll(y, LANES - 32, 1) * sa64 + pltpu.roll(y, 32, 1) * sb64

    def rope128(y):
        return y * c128 + pltpu.roll(y, 64, 1) * s128

    def slices(w_ref):
        n = w_ref.shape[1]
        for c0 in range(0, n, 2 * LANES):
            y = _dot(h, w_ref[:, c0:c0 + 2 * LANES])
            for c in (c0, c0 + LANES):
                yield c, y[:, c - c0:c - c0 + LANES]

    diff_scale = 0.125
    idx_scale = 0.125

    for c, y in slices(w64_ref):
        r = rope64(y)
        if c < dw:
            dq_ref[:, c:c + LANES] = (r * diff_scale).astype(BF16)
        elif c < 2 * dw:
            dkf_ref[:, c - dw:c - dw + LANES] = r
            dkb_ref[:, c - dw:c - dw + LANES] = r.astype(BF16)
        else:
            iq_ref[:, c - 2 * dw:c - 2 * dw + LANES] = r * idx_scale
    for c, y in slices(w128_ref):
        r = rope128(y)
        if c < sw:
            sq_ref[:, c:c + LANES] = r.astype(BF16)
        else:
            skf_ref[:, c - sw:c - sw + LANES] = r
            skb_ref[:, c - sw:c - sw + LANES] = r.astype(BF16)
    for c, y in slices(wv_ref):
        if c < dw:
            dvf_ref[:, c:c + LANES] = y
            dvb_ref[:, c:c + LANES] = y.astype(BF16)
        else:
            svf_ref[:, c - dw:c - dw + LANES] = y
            svb_ref[:, c - dw:c - dw + LANES] = y.astype(BF16)
    y = _dot(h, wikw_ref[...])
    r = rope64(y)
    ikf_ref[...] = r[:, :idx_hd]
    ikb_ref[...] = r[:, :idx_hd].astype(BF16)
    iw_ref[...] = y * (IDX_HEADS ** -0.5)


def _proj(x, g, wts, tabs, *, tm, n_tab, dw, sw, iqw, idx_hd):
    n, d = x.shape
    w64, w128, wv, wikw = wts
    row = lambda i: (i, 0)
    const = lambda i: (0, 0)
    tab = lambda i: (i % n_tab, 0)

    def o(width, dtype):
        return jax.ShapeDtypeStruct((n, width), dtype), pl.BlockSpec((tm, width), row)

    outs = [o(dw, BF16), o(dw, BF16), o(dw, F32), o(iqw, F32),
            o(sw, BF16), o(sw, BF16), o(sw, F32),
            o(dw, BF16), o(dw, F32), o(sw, BF16), o(sw, F32),
            o(idx_hd, BF16), o(idx_hd, F32), o(LANES, F32)]
    return pl.pallas_call(
        functools.partial(_proj_kernel, dw=dw, sw=sw, iqw=iqw, idx_hd=idx_hd),
        grid=(n // tm,),
        in_specs=[pl.BlockSpec((tm, d), row), pl.BlockSpec((1, d), const)]
        + [pl.BlockSpec(w.shape, const) for w in (w64, w128, wv, wikw)]
        + [pl.BlockSpec((tm, LANES), tab)] * 5,
        out_specs=[s for _, s in outs],
        out_shape=[s for s, _ in outs],
        compiler_params=_cparams(1),
        name="proj",
    )(x, g, w64, w128, wv, wikw, *tabs)


def _attn_kernel(*refs, diff, heads, tq, tkn, tkc, ncb, past, t_valid, scale, lam_init):
    it = iter(refs)
    q_ref = next(it)
    kc_ref = vc_ref = bc_ref = bn_ref = lam_ref = sub_ref = qs_ref = None
    if ncb:
        kc_ref, vc_ref = next(it), next(it)
    kn_ref, vn_ref = next(it), next(it)
    if diff:
        lam_ref, sub_ref = next(it), next(it)
    else:
        if ncb:
            bc_ref = next(it)
        bn_ref = next(it)
    o_ref = next(it)
    if diff:
        qs_ref = next(it)
    m_ref, l_ref, acc_ref = next(it), next(it), next(it)

    qi, ki = pl.program_id(1), pl.program_id(2)
    rows = 2 * tq if diff else tq

    @pl.when(ki == 0)
    def _():
        m_ref[...] = jnp.full(m_ref.shape, NEG, F32)
        l_ref[...] = jnp.zeros(l_ref.shape, F32)
        acc_ref[...] = jnp.zeros(acc_ref.shape, F32)
        if diff:
            lane = lax.broadcasted_iota(I32, (tq, LANES), 1)
            for h in range(heads):
                qh = q_ref[:, h * LANES:(h + 1) * LANES]
                zero = jnp.zeros_like(qh)
                qs_ref[h, :tq, :] = jnp.where(lane < LANES // 2, qh, zero)
                qs_ref[h, tq:, :] = jnp.where(lane >= LANES // 2, qh, zero)

    def step(get_k, get_v, tk, mask, bias):
        for h in range(heads):
            qh = qs_ref[h] if diff else q_ref[:, h * LANES:(h + 1) * LANES]
            s = _dot_nt(qh, get_k(h))
            if diff:
                s = jnp.where(mask, s, NEG)
            else:
                s = s * scale + bias
            m_prev = m_ref[h]
            m_new = jnp.maximum(m_prev, jnp.max(s, axis=1, keepdims=True))
            alpha = jnp.exp(m_prev - m_new)
            p = jnp.exp(s - m_new)
            l_ref[h] = alpha * l_ref[h] + jnp.sum(p, axis=1, keepdims=True)
            acc_ref[h] = alpha * acc_ref[h] + _dot(p.astype(BF16), get_v(h))
            m_ref[h] = m_new

    def causal_mask(tk, kpos0, extra_valid):
        r = lax.broadcasted_iota(I32, (rows, tk), 0)
        c = lax.broadcasted_iota(I32, (rows, tk), 1)
        r = jnp.where(r >= tq, r - tq, r) if diff else r
        qpos = past + qi * tq + r
        ok = ((kpos0 + c) >> CHUNK_SHIFT) <= (qpos >> CHUNK_SHIFT)
        if extra_valid is not None:
            ok = ok & (c < extra_valid)
        return ok

    if ncb:
        @pl.when(ki < ncb)
        def _():
            hs = lambda h: slice(h * LANES, (h + 1) * LANES)
            mask = causal_mask(tkc, ki * tkc, None) if diff else None
            bias = None if diff else bc_ref[0].astype(F32)
            step(lambda h: kc_ref[0, 0, :, hs(h)].astype(BF16),
                 lambda h: vc_ref[0, 0, :, hs(h)].astype(BF16), tkc, mask, bias)

    kj = ki - ncb

    @pl.when((kj >= 0) & (kj <= qi))
    def _():
        hs = lambda h: slice(h * LANES, (h + 1) * LANES)
        if diff:
            extra = (t_valid - kj * tkn) if t_valid % tkn else None
            mask = causal_mask(tkn, past + kj * tkn, extra)
            bias = None
        else:
            mask = None
            bias = bn_ref[0].astype(F32)
        step(lambda h: kn_ref[:, hs(h)], lambda h: vn_ref[:, hs(h)], tkn, mask, bias)

    @pl.when(kj == qi)
    def _():
        if diff:
            lf = lam_ref[...]
            lam = (jnp.exp(jnp.sum(lf[0:1] * lf[1:2], axis=1, keepdims=True))
                   - jnp.exp(jnp.sum(lf[2:3] * lf[3:4], axis=1, keepdims=True)) + lam_init)
            g = sub_ref[...]
        for h in range(heads):
            o = acc_ref[h] / l_ref[h]
            if diff:
                o = o[:tq] - lam * o[tq:]
                o = _rms(o, g) * (1.0 - lam_init)
            o_ref[:, h * LANES:(h + 1) * LANES] = o.astype(BF16)


def _attn(q, kn, vn, *, diff, batch, t, tq, tkn, past=0, cache=None, layer=0, tkc=0,
          bias_c=None, bias_n=None, lam=None, subln=None, lam_init=0.0, scale=1.0):
    n, width = q.shape
    heads = width // LANES
    nq = t // tq
    ncb = 0 if cache is None else past // tkc
    tn = kn.shape[0] // batch
    nkn = tn // tkn
    rows = 2 * tq if diff else tq

    new_blk = lambda b, qi, ki: (b * nkn + jnp.clip(ki - ncb, 0, qi), 0)
    in_specs = [pl.BlockSpec((tq, width), lambda b, qi, ki: (b * nq + qi, 0))]
    args = [q]
    if ncb:
        cidx = lambda b, qi, ki: (layer, b, jnp.minimum(ki, ncb - 1), 0)
        in_specs += [pl.BlockSpec((1, 1, tkc, width), cidx)] * 2
        args += list(cache)
    in_specs += [pl.BlockSpec((tkn, width), new_blk)] * 2
    args += [kn, vn]
    if diff:
        in_specs += [pl.BlockSpec(lam.shape, lambda b, qi, ki: (0, 0)),
                     pl.BlockSpec(subln.shape, lambda b, qi, ki: (0, 0))]
        args += [lam, subln]
    else:
        if ncb:
            in_specs.append(pl.BlockSpec((1, tq, tkc), lambda b, qi, ki: (b, qi, jnp.minimum(ki, ncb - 1))))
            args.append(bias_c)
        in_specs.append(pl.BlockSpec((1, tq, tkn), lambda b, qi, ki: (b, qi, jnp.clip(ki - ncb, 0, qi))))
        args.append(bias_n)
    scratch = ([pltpu.VMEM((heads, rows, LANES), BF16)] if diff else []) + [
        pltpu.VMEM((heads, rows, 1), F32), pltpu.VMEM((heads, rows, 1), F32),
        pltpu.VMEM((heads, rows, LANES), F32)]
    return pl.pallas_call(
        functools.partial(_attn_kernel, diff=diff, heads=heads, tq=tq, tkn=tkn, tkc=tkc, ncb=ncb,
                          past=past, t_valid=t, scale=scale, lam_init=lam_init),
        grid=(batch, nq, ncb + nq),
        in_specs=in_specs,
        out_specs=pl.BlockSpec((tq, width), lambda b, qi, ki: (b * nq + qi, 0)),
        out_shape=jax.ShapeDtypeStruct((n, width), BF16),
        scratch_shapes=scratch,
        compiler_params=_cparams(3),
        name="attn_diff" if diff else "attn_dsa",
    )(*args)


def _select_kernel(*refs, tq, t, tn, past, topk, idx_hd, nbits):
    it = iter(refs)
    iq_ref, iw_ref, ikn_ref = next(it), next(it), next(it)
    ikc_ref = next(it) if past else None
    bc_ref = next(it) if past else None
    bn_ref = next(it)
    key_ref, j_ref = next(it), next(it)
    qi = pl.program_id(1)
    ltot = past + tn

    qpos = past + qi * tq + lax.broadcasted_iota(I32, (tq, 1), 0)

    def keys(kmat, kpos, valid):
        sc = jnp.zeros((tq, kmat.shape[0]), F32)
        for h in range(IDX_HEADS):
            qh = iq_ref[:, h * idx_hd:(h + 1) * idx_hd].astype(BF16)
            w = iw_ref[:, idx_hd + h:idx_hd + h + 1]
            sc = sc + w * jnp.maximum(_dot_nt(qh, kmat), 0.0)
        bits = lax.bitcast_convert_type(sc, I32)
        key = jnp.where(bits < 0, bits ^ 0x7FFFFFFF, bits)
        ok = (kpos >> CHUNK_SHIFT) <= (qpos >> CHUNK_SHIFT)
        if valid is not None:
            ok = ok & valid
        return jnp.where(ok, key, KEY_NEG_INF)

    if past:
        kpos = lax.broadcasted_iota(I32, (1, past), 1)
        key_ref[:, :past] = keys(ikc_ref[0, 0].astype(BF16), kpos, None)
    c = lax.broadcasted_iota(I32, (1, tn), 1)
    key_ref[:, past:] = keys(ikn_ref[0], past + c, (c < t) if tn != t else None)

    kf = float(topk)

    def count(pred):
        return jnp.sum(jnp.where(pred, 1.0, 0.0), axis=1, keepdims=True)

    def tbody(i, tu):
        cand = tu | jnp.left_shift(jnp.int32(1), jnp.int32(31) - i)
        return jnp.where(count(key_ref[...] >= (cand ^ INT_MIN)) >= kf, cand, tu)

    thr = lax.fori_loop(0, 32, tbody, jnp.zeros((tq, 1), I32)) ^ INT_MIN

    key = key_ref[...]
    eq = key == thr
    need = kf - count(key > thr)
    excess = (count(eq) > need) & (thr > KEY_NEG_INF)
    j_ref[...] = jnp.full((tq, 1), ltot, I32)

    @pl.when(jnp.max(jnp.where(excess, 1.0, 0.0)) > 0.0)
    def _():
        idx = lax.broadcasted_iota(I32, (tq, ltot), 1)
        eqk = key_ref[...] == thr

        def jbody(i, j):
            cand = j | jnp.left_shift(jnp.int32(1), jnp.int32(nbits - 1) - i)
            return jnp.where(count(eqk & (idx < cand)) < need, cand, j)

        j_ref[...] = lax.fori_loop(0, nbits, jbody, jnp.zeros((tq, 1), I32))

    idx = lax.broadcasted_iota(I32, (tq, ltot), 1)
    sel = (key > thr) | (eq & (idx <= j_ref[...]))
    sel = sel & (key > KEY_NEG_INF) & (key < KEY_POS_INF)
    bias = jnp.where(sel, 0.0, NEG).astype(BF16)
    if past:
        bc_ref[0] = bias[:, :past]
    bn_ref[0] = bias[:, past:]


def _select(iq, iw, ikn, *, batch, t, tq, past=0, ikc=None, layer=0):
    tn, idx_hd = ikn.shape[1], ikn.shape[2]
    nq = t // tq
    ltot = past + tn
    topk = min(TOPK_MAX, (past + t) // 4)
    nbits = max(1, (ltot - 1).bit_length())
    row = lambda b, qi: (b * nq + qi, 0)
    in_specs = [pl.BlockSpec((tq, iq.shape[1]), row), pl.BlockSpec((tq, LANES), row),
                pl.BlockSpec((1, tn, idx_hd), lambda b, qi: (b, 0, 0))]
    args = [iq, iw, ikn]
    out_specs, out_shape = [], []
    if past:
        in_specs.append(pl.BlockSpec((1, 1, past, idx_hd), lambda b, qi: (layer, b, 0, 0)))
        args.append(ikc)
        out_specs.append(pl.BlockSpec((1, tq, past), lambda b, qi: (b, qi, 0)))
        out_shape.append(jax.ShapeDtypeStruct((batch, t, past), BF16))
    out_specs.append(pl.BlockSpec((1, tq, tn), lambda b, qi: (b, qi, 0)))
    out_shape.append(jax.ShapeDtypeStruct((batch, t, tn), BF16))
    return pl.pallas_call(
        functools.partial(_select_kernel, tq=tq, t=t, tn=tn, past=past, topk=topk, idx_hd=idx_hd,
                          nbits=nbits),
        grid=(batch, nq),
        in_specs=in_specs, out_specs=out_specs, out_shape=out_shape,
        scratch_shapes=[pltpu.VMEM((tq, ltot), I32), pltpu.VMEM((tq, 1), I32)],
        compiler_params=_cparams(2),
        name="select",
    )(*args)


def _post_kernel(x_ref, oa_ref, ob_ref, ga_ref, wa_ref, wb_ref, wg_ref, bg_ref, wo_ref, gf_ref,
                 xo_ref, h2_ref, mg_ref):
    d = x_ref.shape[1]
    x = x_ref[...]
    h = _rms(x, ga_ref[...]).astype(BF16)
    oa, ob = oa_ref[...], ob_ref[...]
    cw = 2 * LANES
    for c in range(0, d, cw):
        gate_a = jax.nn.sigmoid(_dot(h, wg_ref[:, c:c + cw]) + bg_ref[:, c:c + cw])
        gate_b = jax.nn.sigmoid(_dot(h, wg_ref[:, d + c:d + c + cw]) + bg_ref[:, d + c:d + c + cw])
        merged = gate_a * _dot(oa, wa_ref[:, c:c + cw]) + gate_b * _dot(ob, wb_ref[:, c:c + cw])
        mg_ref[:, c:c + cw] = merged.astype(BF16)
    mg = mg_ref[...]
    for c in range(0, d, cw):
        xo_ref[:, c:c + cw] = x[:, c:c + cw] + _dot(mg, wo_ref[:, c:c + cw])
    h2_ref[...] = _rms(xo_ref[...], gf_ref[...]).astype(BF16)


def _post(x, oa, ob, ga, wa, wb, wg, bg, wo, gf, *, tm):
    n, d = x.shape
    row = lambda i: (i, 0)
    const = lambda i: (0, 0)
    full = lambda a: pl.BlockSpec(a.shape, const)
    return pl.pallas_call(
        _post_kernel,
        grid=(n // tm,),
        in_specs=[pl.BlockSpec((tm, d), row)] * 3 + [full(a) for a in (ga, wa, wb, wg, bg, wo, gf)],
        out_specs=[pl.BlockSpec((tm, d), row)] * 2,
        out_shape=[jax.ShapeDtypeStruct((n, d), F32), jax.ShapeDtypeStruct((n, d), BF16)],
        scratch_shapes=[pltpu.VMEM((tm, d), BF16)],
        compiler_params=_cparams(1),
        name="post",
    )(x, oa, ob, ga, wa, wb, wg, bg, wo, gf)


def _ffn_kernel(x_ref, h_ref, wl_ref, wgt_ref, cwl_ref, cwg_ref, cbl_ref, cbg_ref, pl_ref, pg_ref,
                wd_ref, xo_ref, nl_ref, ng_ref, acc_ref, cl_ref, cg_ref, *, tm):
    i, j = pl.program_id(1), pl.program_id(2)
    nj = pl.num_programs(2)
    h = h_ref[...]
    rid = lax.broadcasted_iota(I32, (tm, 1), 0)

    def conv(w_ref, cw_ref, cb_ref, prev_ref, carry_ref, new_ref):
        u = _dot(h, w_ref[...])
        hal = jnp.where(i == 0, prev_ref[0], carry_ref[j, 6:8, :])
        u1 = jnp.where(rid == 0, hal[1:2], pltpu.roll(u, 1, 0))
        u2 = jnp.where(rid == 0, hal[0:1], jnp.where(rid == 1, hal[1:2], pltpu.roll(u, 2, 0)))
        cw = cw_ref[...]
        carry_ref[j] = u[tm - 8:, :]
        new_ref[0] = u[tm - 2:, :]
        return cb_ref[...] + u2 * cw[0:1] + u1 * cw[1:2] + u * cw[2:3]

    c_lin = conv(wl_ref, cwl_ref, cbl_ref, pl_ref, cl_ref, nl_ref)
    c_gate = conv(wgt_ref, cwg_ref, cbg_ref, pg_ref, cg_ref, ng_ref)
    act = (jax.nn.silu(c_gate) * c_lin).astype(BF16)
    part = _dot(act, wd_ref[...])

    @pl.when(j == 0)
    def _():
        acc_ref[...] = part

    @pl.when(j > 0)
    def _():
        acc_ref[...] += part

    @pl.when(j == nj - 1)
    def _():
        xo_ref[...] = x_ref[...] + acc_ref[...]


def _ffn(x, h2, w_up, conv_w, conv_b, conv_prev, w_down, *, batch, t, tm, tn):
    n, d = x.shape
    dff = w_down.shape[0]
    nt, nj = t // tm, dff // tn
    row = lambda b, i, j: (b * nt + i, 0)
    lin = lambda b, i, j: (0, j)
    gate = lambda b, i, j: (0, nj + j)
    return pl.pallas_call(
        functools.partial(_ffn_kernel, tm=tm),
        grid=(batch, nt, nj),
        in_specs=[pl.BlockSpec((tm, d), row), pl.BlockSpec((tm, d), row),
                  pl.BlockSpec((d, tn), lin), pl.BlockSpec((d, tn), gate),
                  pl.BlockSpec((CONV_W, tn), lin), pl.BlockSpec((CONV_W, tn), gate),
                  pl.BlockSpec((1, tn), lin), pl.BlockSpec((1, tn), gate),
                  pl.BlockSpec((1, CONV_W - 1, tn), lambda b, i, j: (b, 0, j)),
                  pl.BlockSpec((1, CONV_W - 1, tn), lambda b, i, j: (b, 0, nj + j)),
                  pl.BlockSpec((tn, d), lambda b, i, j: (j, 0))],
        out_specs=[pl.BlockSpec((tm, d), row),
                   pl.BlockSpec((1, CONV_W - 1, tn), lambda b, i, j: (b, 0, j)),
                   pl.BlockSpec((1, CONV_W - 1, tn), lambda b, i, j: (b, 0, j))],
        out_shape=[jax.ShapeDtypeStruct((n, d), F32),
                   jax.ShapeDtypeStruct((batch, CONV_W - 1, dff), F32),
                   jax.ShapeDtypeStruct((batch, CONV_W - 1, dff), F32)],
        scratch_shapes=[pltpu.VMEM((tm, d), F32), pltpu.VMEM((nj, 8, tn), F32),
                        pltpu.VMEM((nj, 8, tn), F32)],
        compiler_params=_cparams(3),
        name="ffn",
    )(x, h2, w_up, w_up, conv_w, conv_w, conv_b, conv_b, conv_prev, conv_prev, w_down)


def _final_kernel(x_ref, g_ref, o_ref):
    o_ref[...] = _rms(x_ref[...], g_ref[...])


def _final_norm(x, g, *, tm):
    n, d = x.shape
    return pl.pallas_call(
        _final_kernel, grid=(n // tm,),
        in_specs=[pl.BlockSpec((tm, d), lambda i: (i, 0)), pl.BlockSpec((1, d), lambda i: (0, 0))],
        out_specs=pl.BlockSpec((tm, d), lambda i: (i, 0)),
        out_shape=jax.ShapeDtypeStruct((n, d), F32),
        compiler_params=_cparams(1), name="final_norm",
    )(x, g)


def _rope_tables(pos, tm):
    def ang(hd):
        half = hd // 2
        inv = ROPE_THETA ** (-(jnp.arange(half, dtype=F32) / half))
        return pos.astype(F32)[:, None] * inv[None, :]

    a64, a128 = ang(64), ang(128)
    cos64 = jnp.tile(jnp.cos(a64), (1, 4))
    sin64 = jnp.tile(jnp.sin(a64), (1, 4))
    lane = jnp.arange(LANES)[None, :]
    low64 = (lane % 64) < 32
    sa64 = jnp.where(low64, -sin64, 0.0)
    sb64 = jnp.where(low64, 0.0, sin64)
    cos128 = jnp.tile(jnp.cos(a128), (1, 2))
    sin128 = jnp.tile(jnp.sin(a128), (1, 2))
    s128 = jnp.where(lane < 64, -sin128, sin128)
    tabs = (cos64, sa64, sb64, cos128, s128)
    reps = -(-tm // pos.shape[0])
    return tuple(jnp.tile(tb, (reps, 1)) for tb in tabs)


def _pick(n, pref):
    t = min(n, pref)
    while n % t:
        t //= 2
    return t


def kernel(x_prompt, x_sample, cache_diff_k, cache_diff_v, cache_dsa_k, cache_dsa_v, cache_idx_k,
           state_ffn_conv, norm_attn, w_in, diff_lambda, diff_subln, w_branch_a, w_branch_b, w_gate,
           b_gate, w_out, norm_ffn, w_up, conv_w, conv_b, w_down, norm_final):
    depth, dec_batch, past = cache_diff_k.shape[:3]
    batch, seq, d_model = x_prompt.shape
    dec_seq = x_sample.shape[1]
    diff_heads2, diff_hd = cache_diff_k.shape[3:]
    dsa_heads, dsa_hd = cache_dsa_k.shape[3:]
    idx_hd = cache_idx_k.shape[3]
    dw, sw, iqw = diff_heads2 * diff_hd, dsa_heads * dsa_hd, IDX_HEADS * idx_hd
    dff = w_down.shape[1]
    assert diff_hd == 64 and idx_hd == 64 and dsa_hd == LANES and 2 * diff_hd == LANES
    assert w_in.shape[2] == 3 * dw + 3 * sw + iqw + idx_hd + IDX_HEADS
    assert seq % CHUNK == 0 and dec_seq % CHUNK == 0 and past % CHUNK == 0

    o = [0]
    for size in (dw, dw, dw, sw, sw, sw, iqw, idx_hd, IDX_HEADS):
        o.append(o[-1] + size)
    wb = w_in.astype(BF16)
    w64 = jnp.concatenate([wb[:, :, o[0]:o[2]], wb[:, :, o[6]:o[7]]], axis=2)
    w128 = wb[:, :, o[3]:o[5]]
    wv = jnp.concatenate([wb[:, :, o[2]:o[3]], wb[:, :, o[5]:o[6]]], axis=2)
    wikw = jnp.pad(wb[:, :, o[7]:o[9]], ((0, 0), (0, 0), (0, LANES - idx_hd - IDX_HEADS)))
    wa_b, wb_b, wg_b, wo_b = (w.astype(BF16) for w in (w_branch_a, w_branch_b, w_gate, w_out))
    wup_b, wdn_b = w_up.astype(BF16), w_down.astype(BF16)
    r2 = lambda a: a.reshape(a.shape[0], 1, a.shape[1])

    cfgs = []
    for name, x, bsz, t, pst in (("p", x_prompt, batch, seq, 0), ("s", x_sample, dec_batch, dec_seq, past)):
        n = bsz * t
        tm = _pick(n, 256)
        tq = _pick(t, 256)
        tkn = max(tq, LANES)
        tn_rows = max(t, tkn)
        pos = pst + jnp.arange(t, dtype=jnp.int32)
        cfgs.append(dict(name=name, x=x.reshape(n, d_model), bsz=bsz, t=t, past=pst, n=n, tm=tm, tq=tq,
                         tkn=tkn, tn_rows=tn_rows, tabs=_rope_tables(pos, tm), n_tab=max(t // tm, 1),
                         tqs=_pick(t, 128), tmf=_pick(t, 512), rows=[]))

    caches = None
    if past:
        flat = lambda a: a.reshape(depth, dec_batch, past, -1)
        caches = dict(dk=flat(cache_diff_k), dv=flat(cache_diff_v), sk=flat(cache_dsa_k),
                      sv=flat(cache_dsa_v))
    tkc = _pick(past, 512) if past else 0

    for l in range(depth):
        lam_init = 0.8 - 0.6 * math.exp(-0.3 * l)
        for cf in cfgs:
            bsz, t, pst, n = cf["bsz"], cf["t"], cf["past"], cf["n"]
            (dq, dkb, dkf, iq, sq, skb, skf, dvb, dvf, svb, svf, ikb, ikf, iw) = _proj(
                cf["x"], r2(norm_attn)[l], (w64[l], w128[l], wv[l], wikw[l]), cf["tabs"],
                tm=cf["tm"], n_tab=cf["n_tab"], dw=dw, sw=sw, iqw=iqw, idx_hd=idx_hd)

            def padrows(a):
                if cf["tn_rows"] == t:
                    return a
                a = a.reshape(bsz, t, a.shape[1])
                a = jnp.pad(a, ((0, 0), (0, cf["tn_rows"] - t), (0, 0)))
                return a.reshape(bsz * cf["tn_rows"], -1)

            has_cache = bool(pst)
            common = dict(batch=bsz, t=t, tq=cf["tq"], tkn=cf["tkn"], past=pst, layer=l, tkc=tkc)
            oa = _attn(dq, padrows(dkb), padrows(dvb), diff=True,
                       cache=(caches["dk"], caches["dv"]) if has_cache else None,
                       lam=diff_lambda[l], subln=r2(diff_subln)[l], lam_init=lam_init, **common)
            ikn = padrows(ikb).reshape(bsz, cf["tn_rows"], idx_hd)
            sel = _select(iq, iw, ikn, batch=bsz, t=t, tq=cf["tqs"], past=pst,
                          ikc=cache_idx_k if has_cache else None, layer=l)
            bias_c, bias_n = (sel[0], sel[1]) if has_cache else (None, sel[0])
            ob = _attn(sq, padrows(skb), padrows(svb), diff=False,
                       cache=(caches["sk"], caches["sv"]) if has_cache else None,
                       bias_c=bias_c, bias_n=bias_n, scale=dsa_hd ** -0.5, **common)
            x_mid, h2 = _post(cf["x"], oa, ob, r2(norm_attn)[l], wa_b[l], wb_b[l], wg_b[l], r2(b_gate)[l],
                              wo_b[l], r2(norm_ffn)[l], tm=cf["tm"])
            prev = state_ffn_conv[l] if has_cache else jnp.zeros((bsz, CONV_W - 1, 2 * dff), F32)
            x_new, new_l, new_g = _ffn(x_mid, h2, wup_b[l], conv_w[l], r2(conv_b)[l], prev, wdn_b[l],
                                       batch=bsz, t=t, tm=cf["tmf"], tn=_pick(dff, 256))
            cf["x"] = x_new
            cf["rows"].append((dkf.reshape(bsz, t, diff_heads2, diff_hd),
                               dvf.reshape(bsz, t, diff_heads2 // 2, 2 * diff_hd),
                               skf.reshape(bsz, t, dsa_heads, dsa_hd),
                               svf.reshape(bsz, t, dsa_heads, dsa_hd),
                               ikf.reshape(bsz, t, idx_hd),
                               jnp.concatenate([new_l, new_g], axis=-1)))

    outs = []
    for cf in cfgs:
        y = _final_norm(cf["x"], norm_final.reshape(1, -1), tm=cf["tm"])
        outs.append(y.reshape(cf["bsz"], cf["t"], d_model))
    stacks = [[jnp.stack(z, axis=0) for z in zip(*cf["rows"])] for cf in cfgs]
    return (outs[0], outs[1], *stacks[0], *stacks[1])
```

```python
import functools
import math

import numpy as np
import jax
import jax.numpy as jnp
from jax import lax
from jax.experimental import pallas as pl
from jax.experimental.pallas import tpu as pltpu

F32 = jnp.float32
BF16 = jnp.bfloat16
I32 = jnp.int32

CHUNK = 64
CHUNK_SHIFT = 6
IDX_HEADS = 4
TOPK_MAX = 256
CONV_W = 3
ROPE_THETA = 10000.0
EPS = 1e-6

LANES = 128
NEG = -1e30
LOG2E = 1.4426950408889634
INT_MIN = -(2 ** 31)
KEY_NEG_INF = INT_MIN + 0x7FFFFF
KEY_POS_INF = 0x7F800000
VMEM_LIMIT = 56 * 1024 * 1024
SEL_WSTEP = 512


def _cparams(n_axes):
    return pltpu.CompilerParams(dimension_semantics=("arbitrary",) * n_axes,
                                vmem_limit_bytes=VMEM_LIMIT)


def _rms(x, g):
    return x * lax.rsqrt(jnp.mean(x * x, axis=-1, keepdims=True) + EPS) * g


def _dot(a, b):
    return jnp.dot(a, b, preferred_element_type=F32)


def _dot_nt(a, b):
    return lax.dot_general(a, b, (((1,), (1,)), ((), ())), preferred_element_type=F32)


def _pick(n, pref):
    t = min(n, pref)
    while n % t:
        t //= 2
    return t


def _proj_kernel(x_ref, g_ref, w64_ref, w128_ref, wv_ref, wikw_ref,
                 c64_ref, sa64_ref, sb64_ref, c128_ref, s128_ref,
                 dq_ref, dkb_ref, dkf_ref, iq_ref, sq_ref, skb_ref, skf_ref,
                 dvb_ref, dvf_ref, svb_ref, svf_ref, ikb_ref, ikf_ref, iw_ref,
                 *, dw, sw, idx_hd):
    h = _rms(x_ref[...], g_ref[...]).astype(BF16)
    c64, sa64, sb64 = c64_ref[...], sa64_ref[...], sb64_ref[...]
    c128, s128 = c128_ref[...], s128_ref[...]

    def rope64(y):
        return y * c64 + pltpu.roll(y, LANES - 32, 1) * sa64 + pltpu.roll(y, 32, 1) * sb64

    def rope128(y):
        return y * c128 + pltpu.roll(y, 64, 1) * s128

    def slices(w_ref):
        n = w_ref.shape[1]
        for c0 in range(0, n, 2 * LANES):
            y = _dot(h, w_ref[:, c0:c0 + 2 * LANES])
            for c in (c0, c0 + LANES):
                yield c, y[:, c - c0:c - c0 + LANES]

    diff_scale = 64 ** -0.5 * LOG2E
    dsa_scale = LANES ** -0.5 * LOG2E
    idx_scale = 0.125

    for c, y in slices(w64_ref):
        r = rope64(y)
        if c < dw:
            dq_ref[:, c:c + LANES] = (r * diff_scale).astype(BF16)
        elif c < 2 * dw:
            dkf_ref[:, c - dw:c - dw + LANES] = r
            dkb_ref[:, c - dw:c - dw + LANES] = r.astype(BF16)
        else:
            iq_ref[:, c - 2 * dw:c - 2 * dw + LANES] = r * idx_scale
    for c, y in slices(w128_ref):
        r = rope128(y)
        if c < sw:
            sq_ref[:, c:c + LANES] = (r * dsa_scale).astype(BF16)
        else:
            skf_ref[:, c - sw:c - sw + LANES] = r
            skb_ref[:, c - sw:c - sw + LANES] = r.astype(BF16)
    for c, y in slices(wv_ref):
        if c < dw:
            dvf_ref[:, c:c + LANES] = y
            dvb_ref[:, c:c + LANES] = y.astype(BF16)
        else:
            svf_ref[:, c - dw:c - dw + LANES] = y
            svb_ref[:, c - dw:c - dw + LANES] = y.astype(BF16)
    y = _dot(h, wikw_ref[...])
    r = rope64(y)
    ikf_ref[...] = r[:, :idx_hd]
    ikb_ref[...] = r[:, :idx_hd].astype(BF16)
    iw_ref[...] = y * (IDX_HEADS ** -0.5)


def _proj(x, g, wts, tabs, *, tm, n_tab, dw, sw, iqw, idx_hd):
    n, d = x.shape
    w64, w128, wv, wikw = wts
    row = lambda i: (i, 0)
    const = lambda i: (0, 0)
    tab = lambda i: (i % n_tab, 0)

    def o(width, dtype):
        return jax.ShapeDtypeStruct((n, width), dtype), pl.BlockSpec((tm, width), row)

    outs = [o(dw, BF16), o(dw, BF16), o(dw, F32), o(iqw, F32),
            o(sw, BF16), o(sw, BF16), o(sw, F32),
            o(dw, BF16), o(dw, F32), o(sw, BF16), o(sw, F32),
            o(idx_hd, BF16), o(idx_hd, F32), o(LANES, F32)]
    return pl.pallas_call(
        functools.partial(_proj_kernel, dw=dw, sw=sw, idx_hd=idx_hd),
        grid=(n // tm,),
        in_specs=[pl.BlockSpec((tm, d), row), pl.BlockSpec((1, d), const)]
        + [pl.BlockSpec(w.shape, const) for w in (w64, w128, wv, wikw)]
        + [pl.BlockSpec((tm, LANES), tab)] * 5,
        out_specs=[s for _, s in outs],
        out_shape=[s for s, _ in outs],
        compiler_params=_cparams(1),
        name="proj",
    )(x, g, w64, w128, wv, wikw, *tabs)


def _attn_kernel(qt_ref, kt_ref, *refs, diff, heads, tq, tkn, tkc, ncb, past, t_valid, lam_init, rc):
    it = iter(refs)
    q_ref = next(it)
    kc_ref = vc_ref = bc_ref = bn_ref = lam_ref = sub_ref = qs_ref = None
    if ncb:
        kc_ref, vc_ref = next(it), next(it)
    kn_ref, vn_ref = next(it), next(it)
    if diff:
        lam_ref, sub_ref = next(it), next(it)
    else:
        if ncb:
            bc_ref = next(it)
        bn_ref = next(it)
    o_ref = next(it)
    if diff:
        qs_ref = next(it)
    m_ref, l_ref, acc_ref, s_ref, p_ref, vx_ref, bias_ref = (next(it) for _ in range(7))

    step_id = pl.program_id(1)
    qi, ki = qt_ref[step_id], kt_ref[step_id]
    rows = 2 * tq if diff else tq
    hs = lambda h: slice(h * LANES, (h + 1) * LANES)

    @pl.when(ki == 0)
    def _():
        m_ref[...] = jnp.full(m_ref.shape, NEG, F32)
        l_ref[...] = jnp.zeros(l_ref.shape, F32)
        acc_ref[...] = jnp.zeros(acc_ref.shape, F32)
        vx_ref[:, :, LANES:] = jnp.ones((heads, vx_ref.shape[1], LANES), BF16)
        if diff:
            lane = lax.broadcasted_iota(I32, (tq, LANES), 1)
            for h in range(heads):
                qh = q_ref[:, hs(h)]
                zero = jnp.zeros_like(qh)
                qs_ref[h, :tq, :] = jnp.where(lane < LANES // 2, qh, zero)
                qs_ref[h, tq:, :] = jnp.where(lane >= LANES // 2, qh, zero)

    def step(get_k, get_v, tk, biased):
        n = 0
        for h in range(heads):
            kh = get_k(h)
            vx_ref[h, :tk, :LANES] = get_v(h)
            for r0 in range(0, rows, rc):
                rs = slice(r0, r0 + rc)
                buf = n % s_ref.shape[0]
                n += 1
                qh = qs_ref[h, rs, :] if diff else q_ref[rs, hs(h)]
                s_ref[buf, :, :tk] = _dot_nt(qh, kh)

                def col(c, buf=buf, r0=r0):
                    x = s_ref[buf, :, c * LANES:(c + 1) * LANES]
                    if biased:
                        b0 = r0 % tq
                        x = x + bias_ref[b0:b0 + rc, c * LANES:(c + 1) * LANES]
                    return x

                mx = col(0)
                for c in range(1, tk // LANES):
                    mx = jnp.maximum(mx, col(c))
                m_prev = m_ref[h, rs, :]
                m_new = jnp.maximum(m_prev, jnp.max(mx, axis=1, keepdims=True))
                alpha = jnp.exp2(m_prev - m_new)
                for c in range(tk // LANES):
                    p_ref[buf, :, c * LANES:(c + 1) * LANES] = jnp.exp2(col(c) - m_new).astype(BF16)
                pv = _dot(p_ref[buf, :, :tk], vx_ref[h, :tk, :])
                acc_ref[h, rs, :] = alpha * acc_ref[h, rs, :] + pv[:, :LANES]
                l_ref[h, rs, :] = alpha * l_ref[h, rs, :] + pv[:, LANES:]
                m_ref[h, rs, :] = m_new

    if ncb:
        @pl.when(ki < ncb)
        def _():
            if not diff:
                bias_ref[:, :tkc] = bc_ref[0].astype(F32)
            step(lambda h: kc_ref[0, 0, :, hs(h)], lambda h: vc_ref[0, 0, :, hs(h)], tkc, not diff)

    kj = ki - ncb
    get_kn, get_vn = (lambda h: kn_ref[:, hs(h)]), (lambda h: vn_ref[:, hs(h)])

    if diff:
        @pl.when((kj >= 0) & (kj < qi))
        def _():
            step(get_kn, get_vn, tkn, False)

        @pl.when(kj == qi)
        def _():
            r = lax.broadcasted_iota(I32, (tq, tkn), 0)
            c = lax.broadcasted_iota(I32, (tq, tkn), 1)
            ok = ((past + kj * tkn + c) >> CHUNK_SHIFT) <= ((past + qi * tq + r) >> CHUNK_SHIFT)
            if t_valid % tkn:
                ok = ok & (c < t_valid - kj * tkn)
            bias_ref[:, :tkn] = jnp.where(ok, 0.0, NEG)
            step(get_kn, get_vn, tkn, True)
    else:
        @pl.when(kj >= 0)
        def _():
            bias_ref[:, :tkn] = bn_ref[0].astype(F32)
            step(get_kn, get_vn, tkn, True)

    @pl.when(kj == qi)
    def _():
        if diff:
            lf = lam_ref[...]
            lam = (jnp.exp(jnp.sum(lf[0:1] * lf[1:2], axis=1, keepdims=True))
                   - jnp.exp(jnp.sum(lf[2:3] * lf[3:4], axis=1, keepdims=True)) + lam_init)
            g = sub_ref[...]
        for h in range(heads):
            o = acc_ref[h] / l_ref[h]
            if diff:
                o = o[:tq] - lam * o[tq:]
                o = _rms(o, g) * (1.0 - lam_init)
            o_ref[:, hs(h)] = o.astype(BF16)


def _attn(q, kn, vn, *, diff, batch, t, tq, tkn, past=0, cache=None, layer=0, tkc=0,
          bias_c=None, bias_n=None, lam=None, subln=None, lam_init=0.0):
    n, width = q.shape
    heads = width // LANES
    nq = t // tq
    ncb = 0 if cache is None else past // tkc
    nkn = kn.shape[0] // batch // tkn
    rows = 2 * tq if diff else tq
    rc = min(tq, 256)
    steps = [(qi, ki) for qi in range(nq) for ki in range(ncb + qi + 1)]
    qt = jnp.asarray(np.array([s[0] for s in steps], np.int32))
    kt = jnp.asarray(np.array([s[1] for s in steps], np.int32))

    def new_idx(b, s, qt, kt):
        return jnp.maximum(kt[s] - ncb, 0)

    qrow = lambda b, s, qt, kt: (b * nq + qt[s], 0)
    new_blk = lambda b, s, qt, kt: (b * nkn + new_idx(b, s, qt, kt), 0)
    const2 = lambda b, s, qt, kt: (0, 0)
    in_specs = [pl.BlockSpec((tq, width), qrow)]
    args = [q]
    if ncb:
        cidx = lambda b, s, qt, kt: (layer, b, jnp.minimum(kt[s], ncb - 1), 0)
        in_specs += [pl.BlockSpec((1, 1, tkc, width), cidx)] * 2
        args += list(cache)
    in_specs += [pl.BlockSpec((tkn, width), new_blk)] * 2
    args += [kn, vn]
    if diff:
        in_specs += [pl.BlockSpec(lam.shape, const2), pl.BlockSpec(subln.shape, const2)]
        args += [lam, subln]
    else:
        if ncb:
            in_specs.append(pl.BlockSpec((1, tq, tkc), lambda b, s, qt, kt: (b, qt[s], jnp.minimum(kt[s], ncb - 1))))
            args.append(bias_c)
        in_specs.append(pl.BlockSpec((1, tq, tkn), lambda b, s, qt, kt: (b, qt[s], new_idx(b, s, qt, kt))))
        args.append(bias_n)
    tkmax = max(tkn, tkc)
    nbuf = min(heads * (rows // rc), max(2, 1024 // rc))
    scratch = ([pltpu.VMEM((heads, rows, LANES), BF16)] if diff else []) + [
        pltpu.VMEM((heads, rows, LANES), F32), pltpu.VMEM((heads, rows, LANES), F32),
        pltpu.VMEM((heads, rows, LANES), F32),
        pltpu.VMEM((nbuf, rc, tkmax), F32), pltpu.VMEM((nbuf, rc, tkmax), BF16),
        pltpu.VMEM((heads, tkmax, 2 * LANES), BF16), pltpu.VMEM((tq, tkmax), F32)]
    return pl.pallas_call(
        functools.partial(_attn_kernel, diff=diff, heads=heads, tq=tq, tkn=tkn, tkc=tkc, ncb=ncb,
                          past=past, t_valid=t, lam_init=lam_init, rc=rc),
        grid_spec=pltpu.PrefetchScalarGridSpec(
            num_scalar_prefetch=2, grid=(batch, len(steps)), in_specs=in_specs,
            out_specs=pl.BlockSpec((tq, width), qrow), scratch_shapes=scratch),
        out_shape=jax.ShapeDtypeStruct((n, width), BF16),
        compiler_params=_cparams(2),
        name="attn_diff" if diff else "attn_dsa",
    )(qt, kt, *args)


def _sort_key(score):
    bits = lax.bitcast_convert_type(score, I32)
    return jnp.where(bits < 0, bits ^ 0x7FFFFFFF, bits)


def _index_scores(iq_ref, iw_ref, kmat, idx_hd):
    sc = jnp.zeros((iq_ref.shape[0], kmat.shape[0]), F32)
    for h in range(IDX_HEADS):
        qh = iq_ref[:, h * idx_hd:(h + 1) * idx_hd].astype(BF16)
        w = iw_ref[:, idx_hd + h:idx_hd + h + 1]
        sc = sc + w * jnp.maximum(_dot_nt(qh, kmat), 0.0)
    return sc


def _topk_threshold(count_ge, tq, kf):
    def tbody(i, tu):
        cand = tu | jnp.left_shift(jnp.int32(1), jnp.int32(31) - i)
        return jnp.where(count_ge(cand ^ INT_MIN) >= kf, cand, tu)

    return lax.fori_loop(0, 32, tbody, jnp.zeros((tq, 1), I32)) ^ INT_MIN


def _tie_cutoff(count_eq_before, need, tq, nbits):
    def jbody(i, j):
        cand = j | jnp.left_shift(jnp.int32(1), jnp.int32(nbits - 1) - i)
        return jnp.where(count_eq_before(cand) < need, cand, j)

    return lax.fori_loop(0, nbits, jbody, jnp.zeros((tq, 1), I32))


def _bias_from_keys(key_ref, j_ref, w, *, tq, topk):
    kf = float(topk)
    count = lambda pred: jnp.sum(jnp.where(pred, 1.0, 0.0), axis=1, keepdims=True)
    thr = _topk_threshold(lambda cs: count(key_ref[:, :w] >= cs), tq, kf)
    key = key_ref[:, :w]
    eq = key == thr
    need = kf - count(key > thr)
    excess = (count(eq) > need) & (thr > KEY_NEG_INF)
    j_ref[...] = jnp.full((tq, 1), w, I32)
    idx = lax.broadcasted_iota(I32, (tq, w), 1)

    @pl.when(jnp.max(jnp.where(excess, 1.0, 0.0)) > 0.0)
    def _():
        j_ref[...] = _tie_cutoff(lambda cand: count((key_ref[:, :w] == thr) & (idx < cand)), need, tq,
                                 max(1, (w - 1).bit_length()))

    sel = ((key > thr) | (eq & (idx <= j_ref[...]))) & (key > KEY_NEG_INF) & (key < KEY_POS_INF)
    return jnp.where(sel, 0.0, NEG).astype(BF16)


def _select_kernel(*refs, tq, t, tn, past, topk, idx_hd, wstep):
    it = iter(refs)
    iq_ref, iw_ref, ikn_ref = next(it), next(it), next(it)
    ikc_ref = next(it) if past else None
    bc_ref = next(it) if past else None
    bn_ref, key_ref, j_ref = next(it), next(it), next(it)
    qi = pl.program_id(1)
    qpos = past + qi * tq + lax.broadcasted_iota(I32, (tq, 1), 0)

    def keys(kmat, kpos, valid):
        ok = (kpos >> CHUNK_SHIFT) <= (qpos >> CHUNK_SHIFT)
        if valid is not None:
            ok = ok & valid
        return jnp.where(ok, _sort_key(_index_scores(iq_ref, iw_ref, kmat, idx_hd)), KEY_NEG_INF)

    def run(wn):
        if past:
            key_ref[:, :past] = keys(ikc_ref[0, 0].astype(BF16), lax.broadcasted_iota(I32, (1, past), 1), None)
        c = lax.broadcasted_iota(I32, (1, wn), 1)
        key_ref[:, past:past + wn] = keys(ikn_ref[0, :wn, :], past + c, (c < t) if tn != t else None)
        bias = _bias_from_keys(key_ref, j_ref, past + wn, tq=tq, topk=topk)
        if past:
            bc_ref[0] = bias[:, :past]
        bn_ref[0, :, :wn] = bias[:, past:]
        if wn < tn:
            bn_ref[0, :, wn:] = jnp.full((tq, tn - wn), NEG, BF16)

    ncls = tn // wstep
    if ncls <= 1:
        run(tn)
    else:
        cls = ((qi + 1) * tq + wstep - 1) // wstep - 1
        for j in range(ncls):
            pl.when(cls == j)(functools.partial(run, (j + 1) * wstep))


def _select(iq, iw, ikn, *, batch, t, tq, past=0, ikc=None, layer=0):
    tn, idx_hd = ikn.shape[1], ikn.shape[2]
    nq = t // tq
    topk = min(TOPK_MAX, (past + t) // 4)
    wstep = SEL_WSTEP if (past == 0 and tn == t and tn % SEL_WSTEP == 0 and SEL_WSTEP % tq == 0) else tn
    row = lambda b, qi: (b * nq + qi, 0)
    in_specs = [pl.BlockSpec((tq, iq.shape[1]), row), pl.BlockSpec((tq, LANES), row),
                pl.BlockSpec((1, tn, idx_hd), lambda b, qi: (b, 0, 0))]
    args = [iq, iw, ikn]
    out_specs, out_shape = [], []
    if past:
        in_specs.append(pl.BlockSpec((1, 1, past, idx_hd), lambda b, qi: (layer, b, 0, 0)))
        args.append(ikc)
        out_specs.append(pl.BlockSpec((1, tq, past), lambda b, qi: (b, qi, 0)))
        out_shape.append(jax.ShapeDtypeStruct((batch, t, past), BF16))
    out_specs.append(pl.BlockSpec((1, tq, tn), lambda b, qi: (b, qi, 0)))
    out_shape.append(jax.ShapeDtypeStruct((batch, t, tn), BF16))
    out = pl.pallas_call(
        functools.partial(_select_kernel, tq=tq, t=t, tn=tn, past=past, topk=topk, idx_hd=idx_hd, wstep=wstep),
        grid=(batch, nq),
        in_specs=in_specs, out_specs=out_specs, out_shape=out_shape,
        scratch_shapes=[pltpu.VMEM((tq, past + tn), I32), pltpu.VMEM((tq, 1), I32)],
        compiler_params=_cparams(2),
        name="select",
    )(*args)
    return (out[0], out[1]) if past else (None, out[0])


def _post_kernel(x_ref, oa_ref, ob_ref, ga_ref, wa_ref, wb_ref, wg_ref, bg_ref, wo_ref, gf_ref,
                 xo_ref, h2_ref, mg_ref):
    d = x_ref.shape[1]
    x = x_ref[...]
    h = _rms(x, ga_ref[...]).astype(BF16)
    oa, ob = oa_ref[...], ob_ref[...]
    cw = 2 * LANES
    for c in range(0, d, cw):
        gate_a = jax.nn.sigmoid(_dot(h, wg_ref[:, c:c + cw]) + bg_ref[:, c:c + cw])
        gate_b = jax.nn.sigmoid(_dot(h, wg_ref[:, d + c:d + c + cw]) + bg_ref[:, d + c:d + c + cw])
        merged = gate_a * _dot(oa, wa_ref[:, c:c + cw]) + gate_b * _dot(ob, wb_ref[:, c:c + cw])
        mg_ref[:, c:c + cw] = merged.astype(BF16)
    mg = mg_ref[...]
    for c in range(0, d, cw):
        xo_ref[:, c:c + cw] = x[:, c:c + cw] + _dot(mg, wo_ref[:, c:c + cw])
    h2_ref[...] = _rms(xo_ref[...], gf_ref[...]).astype(BF16)


def _post(x, oa, ob, ga, wa, wb, wg, bg, wo, gf, *, tm):
    n, d = x.shape
    row = lambda i: (i, 0)
    const = lambda i: (0, 0)
    full = lambda a: pl.BlockSpec(a.shape, const)
    return pl.pallas_call(
        _post_kernel,
        grid=(n // tm,),
        in_specs=[pl.BlockSpec((tm, d), row)] * 3 + [full(a) for a in (ga, wa, wb, wg, bg, wo, gf)],
        out_specs=[pl.BlockSpec((tm, d), row)] * 2,
        out_shape=[jax.ShapeDtypeStruct((n, d), F32), jax.ShapeDtypeStruct((n, d), BF16)],
        scratch_shapes=[pltpu.VMEM((tm, d), BF16)],
        compiler_params=_cparams(1),
        name="post",
    )(x, oa, ob, ga, wa, wb, wg, bg, wo, gf)


def _ffn_kernel(x_ref, h_ref, wl_ref, wgt_ref, cwl_ref, cwg_ref, cbl_ref, cbg_ref, pl_ref, pg_ref,
                wd_ref, xo_ref, nl_ref, ng_ref, acc_ref, cl_ref, cg_ref, *, tm):
    i, j = pl.program_id(1), pl.program_id(2)
    nj = pl.num_programs(2)
    h = h_ref[...]
    rid = lax.broadcasted_iota(I32, (tm, 1), 0)

    def conv(w_ref, cw_ref, cb_ref, prev_ref, carry_ref, new_ref):
        u = _dot(h, w_ref[...])
        hal = jnp.where(i == 0, prev_ref[0], carry_ref[j, 6:8, :])
        u1 = jnp.where(rid == 0, hal[1:2], pltpu.roll(u, 1, 0))
        u2 = jnp.where(rid == 0, hal[0:1], jnp.where(rid == 1, hal[1:2], pltpu.roll(u, 2, 0)))
        cw = cw_ref[...]
        carry_ref[j] = u[tm - 8:, :]
        new_ref[0] = u[tm - 2:, :]
        return cb_ref[...] + u2 * cw[0:1] + u1 * cw[1:2] + u * cw[2:3]

    c_lin = conv(wl_ref, cwl_ref, cbl_ref, pl_ref, cl_ref, nl_ref)
    c_gate = conv(wgt_ref, cwg_ref, cbg_ref, pg_ref, cg_ref, ng_ref)
    act = (jax.nn.silu(c_gate) * c_lin).astype(BF16)
    part = _dot(act, wd_ref[...])

    @pl.when(j == 0)
    def _():
        acc_ref[...] = part

    @pl.when(j > 0)
    def _():
        acc_ref[...] += part

    @pl.when(j == nj - 1)
    def _():
        xo_ref[...] = x_ref[...] + acc_ref[...]


def _ffn(x, h2, w_up, conv_w, conv_b, conv_prev, w_down, *, batch, t, tm, tn):
    n, d = x.shape
    dff = w_down.shape[0]
    nt, nj = t // tm, dff // tn
    row = lambda b, i, j: (b * nt + i, 0)
    lin = lambda b, i, j: (0, j)
    gate = lambda b, i, j: (0, nj + j)
    tile_rows = lambda b, i, j: (b * nt + i, 0, j)
    return pl.pallas_call(
        functools.partial(_ffn_kernel, tm=tm),
        grid=(batch, nt, nj),
        in_specs=[pl.BlockSpec((tm, d), row), pl.BlockSpec((tm, d), row),
                  pl.BlockSpec((d, tn), lin), pl.BlockSpec((d, tn), gate),
                  pl.BlockSpec((CONV_W, tn), lin), pl.BlockSpec((CONV_W, tn), gate),
                  pl.BlockSpec((1, tn), lin), pl.BlockSpec((1, tn), gate),
                  pl.BlockSpec((1, CONV_W - 1, tn), lambda b, i, j: (b, 0, j)),
                  pl.BlockSpec((1, CONV_W - 1, tn), lambda b, i, j: (b, 0, nj + j)),
                  pl.BlockSpec((tn, d), lambda b, i, j: (j, 0))],
        out_specs=[pl.BlockSpec((tm, d), row),
                   pl.BlockSpec((1, CONV_W - 1, tn), tile_rows),
                   pl.BlockSpec((1, CONV_W - 1, tn), tile_rows)],
        out_shape=[jax.ShapeDtypeStruct((n, d), F32),
                   jax.ShapeDtypeStruct((batch * nt, CONV_W - 1, dff), F32),
                   jax.ShapeDtypeStruct((batch * nt, CONV_W - 1, dff), F32)],
        scratch_shapes=[pltpu.VMEM((tm, d), F32), pltpu.VMEM((nj, 8, tn), F32),
                        pltpu.VMEM((nj, 8, tn), F32)],
        compiler_params=_cparams(3),
        name="ffn",
    )(x, h2, w_up, w_up, conv_w, conv_w, conv_b, conv_b, conv_prev, conv_prev, w_down)


def _final_kernel(x_ref, g_ref, o_ref):
    o_ref[...] = _rms(x_ref[...], g_ref[...])


def _final_norm(x, g, *, tm):
    n, d = x.shape
    return pl.pallas_call(
        _final_kernel, grid=(n // tm,),
        in_specs=[pl.BlockSpec((tm, d), lambda i: (i, 0)), pl.BlockSpec((1, d), lambda i: (0, 0))],
        out_specs=pl.BlockSpec((tm, d), lambda i: (i, 0)),
        out_shape=jax.ShapeDtypeStruct((n, d), F32),
        compiler_params=_cparams(1), name="final_norm",
    )(x, g)


def _rope_tables(pos, tm):
    def ang(hd):
        half = hd // 2
        inv = ROPE_THETA ** (-(jnp.arange(half, dtype=F32) / half))
        return pos.astype(F32)[:, None] * inv[None, :]

    a64, a128 = ang(64), ang(128)
    cos64 = jnp.tile(jnp.cos(a64), (1, 4))
    sin64 = jnp.tile(jnp.sin(a64), (1, 4))
    lane = jnp.arange(LANES)[None, :]
    low64 = (lane % 64) < 32
    sa64 = jnp.where(low64, -sin64, 0.0)
    sb64 = jnp.where(low64, 0.0, sin64)
    cos128 = jnp.tile(jnp.cos(a128), (1, 2))
    sin128 = jnp.tile(jnp.sin(a128), (1, 2))
    s128 = jnp.where(lane < 64, -sin128, sin128)
    tabs = (cos64, sa64, sb64, cos128, s128)
    reps = -(-tm // pos.shape[0])
    return tuple(jnp.tile(tb, (reps, 1)) for tb in tabs)


def _ffn_chunk(dff):
    best = LANES
    for k in range(1, dff // LANES + 1):
        if dff % (k * LANES) == 0 and k * LANES <= 1536:
            best = k * LANES
    return best


def kernel(x_prompt, x_sample, cache_diff_k, cache_diff_v, cache_dsa_k, cache_dsa_v, cache_idx_k,
           state_ffn_conv, norm_attn, w_in, diff_lambda, diff_subln, w_branch_a, w_branch_b, w_gate,
           b_gate, w_out, norm_ffn, w_up, conv_w, conv_b, w_down, norm_final):
    depth, dec_batch, past = cache_diff_k.shape[:3]
    batch, seq, d_model = x_prompt.shape
    dec_seq = x_sample.shape[1]
    diff_heads2, diff_hd = cache_diff_k.shape[3:]
    dsa_heads, dsa_hd = cache_dsa_k.shape[3:]
    idx_hd = cache_idx_k.shape[3]
    dw, sw, iqw = diff_heads2 * diff_hd, dsa_heads * dsa_hd, IDX_HEADS * idx_hd
    dff = w_down.shape[1]
    assert diff_hd == 64 and idx_hd == 64 and dsa_hd == LANES and 2 * diff_hd == LANES
    assert w_in.shape[2] == 3 * dw + 3 * sw + iqw + idx_hd + IDX_HEADS
    assert seq % CHUNK == 0 and dec_seq % CHUNK == 0 and past % CHUNK == 0

    o = [0]
    for size in (dw, dw, dw, sw, sw, sw, iqw, idx_hd, IDX_HEADS):
        o.append(o[-1] + size)
    wb = w_in.astype(BF16)
    w64 = jnp.concatenate([wb[:, :, o[0]:o[2]], wb[:, :, o[6]:o[7]]], axis=2)
    w128 = wb[:, :, o[3]:o[5]]
    wv = jnp.concatenate([wb[:, :, o[2]:o[3]], wb[:, :, o[5]:o[6]]], axis=2)
    wikw = jnp.pad(wb[:, :, o[7]:o[9]], ((0, 0), (0, 0), (0, LANES - idx_hd - IDX_HEADS)))
    wa_b, wb_b, wg_b, wo_b = (w.astype(BF16) for w in (w_branch_a, w_branch_b, w_gate, w_out))
    wup_b, wdn_b = w_up.astype(BF16), w_down.astype(BF16)
    r2 = lambda a: a.reshape(a.shape[0], 1, a.shape[1])

    cfgs = []
    for name, x, bsz, t, pst in (("p", x_prompt, batch, seq, 0), ("s", x_sample, dec_batch, dec_seq, past)):
        n = bsz * t
        tm = _pick(n, 256)
        tq = _pick(t, 512)
        tkn = max(tq, LANES)
        pos = pst + jnp.arange(t, dtype=jnp.int32)
        cfgs.append(dict(name=name, x=x.reshape(n, d_model), bsz=bsz, t=t, past=pst, n=n, tm=tm, tq=tq,
                         tkn=tkn, tn_rows=max(t, tkn), tabs=_rope_tables(pos, tm), n_tab=max(t // tm, 1),
                         tqs=_pick(t, 512), tmf=_pick(t, 512), rows=[]))

    caches = None
    if past:
        flat = lambda a: a.astype(BF16).reshape(depth, dec_batch, past, -1)
        caches = dict(dk=flat(cache_diff_k), dv=flat(cache_diff_v), sk=flat(cache_dsa_k),
                      sv=flat(cache_dsa_v))
    tkc = _pick(past, 1024) if past else 0
    tn_ffn = _ffn_chunk(dff)

    for l in range(depth):
        lam_init = 0.8 - 0.6 * math.exp(-0.3 * l)
        for cf in cfgs:
            bsz, t, pst, n = cf["bsz"], cf["t"], cf["past"], cf["n"]
            (dq, dkb, dkf, iq, sq, skb, skf, dvb, dvf, svb, svf, ikb, ikf, iw) = _proj(
                cf["x"], r2(norm_attn)[l], (w64[l], w128[l], wv[l], wikw[l]), cf["tabs"],
                tm=cf["tm"], n_tab=cf["n_tab"], dw=dw, sw=sw, iqw=iqw, idx_hd=idx_hd)

            def padrows(a):
                if cf["tn_rows"] == t:
                    return a
                a = a.reshape(bsz, t, a.shape[1])
                a = jnp.pad(a, ((0, 0), (0, cf["tn_rows"] - t), (0, 0)))
                return a.reshape(bsz * cf["tn_rows"], -1)

            has_cache = bool(pst)
            common = dict(batch=bsz, t=t, tq=cf["tq"], tkn=cf["tkn"], past=pst, layer=l, tkc=tkc)
            oa = _attn(dq, padrows(dkb), padrows(dvb), diff=True,
                       cache=(caches["dk"], caches["dv"]) if has_cache else None,
                       lam=diff_lambda[l], subln=r2(diff_subln)[l], lam_init=lam_init, **common)
            ikn = padrows(ikb).reshape(bsz, cf["tn_rows"], idx_hd)
            bias_c, bias_n = _select(iq, iw, ikn, batch=bsz, t=t, tq=cf["tqs"], past=pst,
                                     ikc=cache_idx_k if has_cache else None, layer=l)
            ob = _attn(sq, padrows(skb), padrows(svb), diff=False,
                       cache=(caches["sk"], caches["sv"]) if has_cache else None,
                       bias_c=bias_c, bias_n=bias_n, **common)
            x_mid, h2 = _post(cf["x"], oa, ob, r2(norm_attn)[l], wa_b[l], wb_b[l], wg_b[l], r2(b_gate)[l],
                              wo_b[l], r2(norm_ffn)[l], tm=cf["tm"])
            prev = state_ffn_conv[l] if has_cache else jnp.zeros((bsz, CONV_W - 1, 2 * dff), F32)
            x_new, new_l, new_g = _ffn(x_mid, h2, wup_b[l], conv_w[l], r2(conv_b)[l], prev, wdn_b[l],
                                       batch=bsz, t=t, tm=cf["tmf"], tn=tn_ffn)
            last_tile = lambda a: a.reshape(bsz, t // cf["tmf"], CONV_W - 1, dff)[:, -1]
            cf["x"] = x_new
            cf["rows"].append((dkf.reshape(bsz, t, diff_heads2, diff_hd),
                               dvf.reshape(bsz, t, diff_heads2 // 2, 2 * diff_hd),
                               skf.reshape(bsz, t, dsa_heads, dsa_hd),
                               svf.reshape(bsz, t, dsa_heads, dsa_hd),
                               ikf.reshape(bsz, t, idx_hd),
                               jnp.concatenate([last_tile(new_l), last_tile(new_g)], axis=-1)))

    outs = []
    for cf in cfgs:
        y = _final_norm(cf["x"], norm_final.reshape(1, -1), tm=cf["tm"])
        outs.append(y.reshape(cf["bsz"], cf["t"], d_model))
    stacks = [[jnp.stack(z, axis=0) for z in zip(*cf["rows"])] for cf in cfgs]
    return (outs[0], outs[1], *stacks[0], *stacks[1])
```

```python
import functools
import math

import numpy as np
import jax
import jax.numpy as jnp
from jax import lax
from jax.experimental import pallas as pl
from jax.experimental.pallas import tpu as pltpu

F32 = jnp.float32
BF16 = jnp.bfloat16
I32 = jnp.int32

CHUNK = 64
CHUNK_SHIFT = 6
IDX_HEADS = 4
TOPK_MAX = 256
CONV_W = 3
ROPE_THETA = 10000.0
EPS = 1e-6

LANES = 128
NEG = -1e30
LOG2E = 1.4426950408889634
INT_MIN = -(2 ** 31)
KEY_NEG_INF = INT_MIN + 0x7FFFFF
KEY_POS_INF = 0x7F800000
VMEM_LIMIT = 56 * 1024 * 1024
SEL_WSTEP = 512


def _cparams(n_axes):
    return pltpu.CompilerParams(dimension_semantics=("arbitrary",) * n_axes,
                                vmem_limit_bytes=VMEM_LIMIT)


def _rms(x, g):
    return x * lax.rsqrt(jnp.mean(x * x, axis=-1, keepdims=True) + EPS) * g


def _dot(a, b):
    return jnp.dot(a, b, preferred_element_type=F32)


def _dot_nt(a, b):
    return lax.dot_general(a, b, (((1,), (1,)), ((), ())), preferred_element_type=F32)


def _pick(n, pref):
    t = min(n, pref)
    while n % t:
        t //= 2
    return t


def _proj_kernel(x_ref, g_ref, w64_ref, w128_ref, wv_ref, wikw_ref,
                 c64_ref, sa64_ref, sb64_ref, c128_ref, s128_ref,
                 dq_ref, dkb_ref, dkf_ref, iq_ref, sq_ref, skb_ref, skf_ref,
                 dvb_ref, dvf_ref, svb_ref, svf_ref, ikb_ref, ikf_ref, iw_ref,
                 *, dw, sw, idx_hd):
    h = _rms(x_ref[...], g_ref[...]).astype(BF16)
    c64, sa64, sb64 = c64_ref[...], sa64_ref[...], sb64_ref[...]
    c128, s128 = c128_ref[...], s128_ref[...]

    def rope64(y):
        return y * c64 + pltpu.roll(y, LANES - 32, 1) * sa64 + pltpu.roll(y, 32, 1) * sb64

    def rope128(y):
        return y * c128 + pltpu.roll(y, 64, 1) * s128

    def slices(w_ref):
        n = w_ref.shape[1]
        for c0 in range(0, n, 2 * LANES):
            y = _dot(h, w_ref[:, c0:c0 + 2 * LANES])
            for c in (c0, c0 + LANES):
                yield c, y[:, c - c0:c - c0 + LANES]

    diff_scale = 64 ** -0.5 * LOG2E
    dsa_scale = LANES ** -0.5 * LOG2E
    idx_scale = 0.125

    for c, y in slices(w64_ref):
        r = rope64(y)
        if c < dw:
            dq_ref[:, c:c + LANES] = (r * diff_scale).astype(BF16)
        elif c < 2 * dw:
            dkf_ref[:, c - dw:c - dw + LANES] = r
            dkb_ref[:, c - dw:c - dw + LANES] = r.astype(BF16)
        else:
            iq_ref[:, c - 2 * dw:c - 2 * dw + LANES] = r * idx_scale
    for c, y in slices(w128_ref):
        r = rope128(y)
        if c < sw:
            sq_ref[:, c:c + LANES] = (r * dsa_scale).astype(BF16)
        else:
            skf_ref[:, c - sw:c - sw + LANES] = r
            skb_ref[:, c - sw:c - sw + LANES] = r.astype(BF16)
    for c, y in slices(wv_ref):
        if c < dw:
            dvf_ref[:, c:c + LANES] = y
            dvb_ref[:, c:c + LANES] = y.astype(BF16)
        else:
            svf_ref[:, c - dw:c - dw + LANES] = y
            svb_ref[:, c - dw:c - dw + LANES] = y.astype(BF16)
    y = _dot(h, wikw_ref[...])
    r = rope64(y)
    ikf_ref[...] = r[:, :idx_hd]
    ikb_ref[...] = r[:, :idx_hd].astype(BF16)
    iw_ref[...] = y * (IDX_HEADS ** -0.5)


def _proj(x, g, wts, tabs, *, tm, n_tab, dw, sw, iqw, idx_hd):
    n, d = x.shape
    w64, w128, wv, wikw = wts
    row = lambda i: (i, 0)
    const = lambda i: (0, 0)
    tab = lambda i: (i % n_tab, 0)

    def o(width, dtype):
        return jax.ShapeDtypeStruct((n, width), dtype), pl.BlockSpec((tm, width), row)

    outs = [o(dw, BF16), o(dw, BF16), o(dw, F32), o(iqw, F32),
            o(sw, BF16), o(sw, BF16), o(sw, F32),
            o(dw, BF16), o(dw, F32), o(sw, BF16), o(sw, F32),
            o(idx_hd, BF16), o(idx_hd, F32), o(LANES, F32)]
    return pl.pallas_call(
        functools.partial(_proj_kernel, dw=dw, sw=sw, idx_hd=idx_hd),
        grid=(n // tm,),
        in_specs=[pl.BlockSpec((tm, d), row), pl.BlockSpec((1, d), const)]
        + [pl.BlockSpec(w.shape, const) for w in (w64, w128, wv, wikw)]
        + [pl.BlockSpec((tm, LANES), tab)] * 5,
        out_specs=[s for _, s in outs],
        out_shape=[s for s, _ in outs],
        compiler_params=_cparams(1),
        name="proj",
    )(x, g, w64, w128, wv, wikw, *tabs)


def _attn_kernel(qt_ref, kt_ref, *refs, diff, heads, tq, tkn, tkc, ncb, past, t_valid, lam_init, rc):
    it = iter(refs)
    q_ref = next(it)
    kc_ref = vc_ref = bc_ref = bn_ref = lam_ref = sub_ref = qs_ref = q64_ref = None
    if ncb:
        kc_ref, vc_ref = next(it), next(it)
    kn_ref, vn_ref = next(it), next(it)
    if diff:
        lam_ref, sub_ref = next(it), next(it)
    else:
        if ncb:
            bc_ref = next(it)
        bn_ref = next(it)
    o_ref = next(it)
    if diff:
        qs_ref = next(it)
        q64_ref = next(it) if ncb else None
    m_ref, l_ref, acc_ref, s_ref, p_ref, vx_ref, bias_ref = (next(it) for _ in range(7))

    step_id = pl.program_id(1)
    qi, ki = qt_ref[step_id], kt_ref[step_id]
    rows = 2 * tq if diff else tq
    hs = lambda h: slice(h * LANES, (h + 1) * LANES)

    @pl.when(ki == 0)
    def _():
        m_ref[...] = jnp.full(m_ref.shape, NEG, F32)
        l_ref[...] = jnp.zeros(l_ref.shape, F32)
        acc_ref[...] = jnp.zeros(acc_ref.shape, F32)
        vx_ref[:, :, LANES:] = jnp.ones((heads, vx_ref.shape[1], LANES), BF16)
        if diff:
            lane = lax.broadcasted_iota(I32, (tq, LANES), 1)
            for h in range(heads):
                qh = q_ref[:, hs(h)]
                zero = jnp.zeros_like(qh)
                qs_ref[h, :tq, :] = jnp.where(lane < LANES // 2, qh, zero)
                qs_ref[h, tq:, :] = jnp.where(lane >= LANES // 2, qh, zero)
                if ncb:
                    qf = qh.astype(F32)
                    q64_ref[2 * h] = qf[:, :LANES // 2].astype(BF16)
                    q64_ref[2 * h + 1] = qf[:, LANES // 2:].astype(BF16)

    def new_qk(get_k):
        def qk(h, r0):
            qh = qs_ref[h, r0:r0 + rc, :] if diff else q_ref[r0:r0 + rc, hs(h)]
            return qh, h, functools.partial(get_k, h)
        return qk

    def step(qk, get_v, tk, biased):
        n = 0
        for h in range(heads):
            vx_ref[h, :tk, :LANES] = get_v(h)
            kmemo = {}
            for r0 in range(0, rows, rc):
                rs = slice(r0, r0 + rc)
                buf = n % s_ref.shape[0]
                n += 1
                qh, kkey, kfn = qk(h, r0)
                if kkey not in kmemo:
                    kmemo[kkey] = kfn()
                s_ref[buf, :, :tk] = _dot_nt(qh, kmemo[kkey])

                def col(c, buf=buf, r0=r0):
                    x = s_ref[buf, :, c * LANES:(c + 1) * LANES]
                    if biased:
                        b0 = r0 % tq
                        x = x + bias_ref[b0:b0 + rc, c * LANES:(c + 1) * LANES]
                    return x

                mx = col(0)
                for c in range(1, tk // LANES):
                    mx = jnp.maximum(mx, col(c))
                m_prev = m_ref[h, rs, :]
                m_new = jnp.maximum(m_prev, jnp.max(mx, axis=1, keepdims=True))
                alpha = jnp.exp2(m_prev - m_new)
                for c in range(tk // LANES):
                    p_ref[buf, :, c * LANES:(c + 1) * LANES] = jnp.exp2(col(c) - m_new).astype(BF16)
                pv = _dot(p_ref[buf, :, :tk], vx_ref[h, :tk, :])
                acc_ref[h, rs, :] = alpha * acc_ref[h, rs, :] + pv[:, :LANES]
                l_ref[h, rs, :] = alpha * l_ref[h, rs, :] + pv[:, LANES:]
                m_ref[h, rs, :] = m_new

    if ncb:
        @pl.when(ki < ncb)
        def _():
            if not diff:
                bias_ref[:, :tkc] = bc_ref[0].astype(F32)

            def cached(ref, i):
                n = ref.shape[2] // tkc
                return ref[0, 0, pl.ds(i, tkc, stride=n), :].astype(BF16)

            def qk(h, r0):
                if diff:
                    g, b0 = 2 * h + r0 // tq, r0 % tq
                    return q64_ref[g, b0:b0 + rc, :], g, functools.partial(cached, kc_ref, g)
                return q_ref[r0:r0 + rc, hs(h)], h, functools.partial(cached, kc_ref, h)

            step(qk, functools.partial(cached, vc_ref), tkc, not diff)

    kj = ki - ncb
    get_kn, get_vn = new_qk(lambda h: kn_ref[:, hs(h)]), (lambda h: vn_ref[:, hs(h)])

    if diff:
        @pl.when((kj >= 0) & (kj < qi))
        def _():
            step(get_kn, get_vn, tkn, False)

        @pl.when(kj == qi)
        def _():
            r = lax.broadcasted_iota(I32, (tq, tkn), 0)
            c = lax.broadcasted_iota(I32, (tq, tkn), 1)
            ok = ((past + kj * tkn + c) >> CHUNK_SHIFT) <= ((past + qi * tq + r) >> CHUNK_SHIFT)
            if t_valid % tkn:
                ok = ok & (c < t_valid - kj * tkn)
            bias_ref[:, :tkn] = jnp.where(ok, 0.0, NEG)
            step(get_kn, get_vn, tkn, True)
    else:
        @pl.when(kj >= 0)
        def _():
            bias_ref[:, :tkn] = bn_ref[0].astype(F32)
            step(get_kn, get_vn, tkn, True)

    @pl.when(kj == qi)
    def _():
        if diff:
            lf = lam_ref[...]
            lam = (jnp.exp(jnp.sum(lf[0:1] * lf[1:2], axis=1, keepdims=True))
                   - jnp.exp(jnp.sum(lf[2:3] * lf[3:4], axis=1, keepdims=True)) + lam_init)
            g = sub_ref[...]
        for h in range(heads):
            o = acc_ref[h] / l_ref[h]
            if diff:
                o = o[:tq] - lam * o[tq:]
                o = _rms(o, g) * (1.0 - lam_init)
            o_ref[:, hs(h)] = o.astype(BF16)


def _attn(q, kn, vn, *, diff, batch, t, tq, tkn, past=0, cache=None, layer=0, tkc=0,
          bias_c=None, bias_n=None, lam=None, subln=None, lam_init=0.0):
    n, width = q.shape
    heads = width // LANES
    nq = t // tq
    ncb = 0 if cache is None else past // tkc
    nkn = kn.shape[0] // batch // tkn
    rows = 2 * tq if diff else tq
    rc = min(tq, 256)
    steps = [(qi, ki) for qi in range(nq) for ki in range(ncb + qi + 1)]
    qt = jnp.asarray(np.array([s[0] for s in steps], np.int32))
    kt = jnp.asarray(np.array([s[1] for s in steps], np.int32))

    def new_idx(b, s, qt, kt):
        return jnp.maximum(kt[s] - ncb, 0)

    qrow = lambda b, s, qt, kt: (b * nq + qt[s], 0)
    new_blk = lambda b, s, qt, kt: (b * nkn + new_idx(b, s, qt, kt), 0)
    const2 = lambda b, s, qt, kt: (0, 0)
    in_specs = [pl.BlockSpec((tq, width), qrow)]
    args = [q]
    if ncb:
        cidx = lambda b, s, qt, kt: (layer, b, jnp.minimum(kt[s], ncb - 1), 0)
        in_specs += [pl.BlockSpec((1, 1, c.shape[2] // ncb, c.shape[3]), cidx) for c in cache]
        args += list(cache)
    in_specs += [pl.BlockSpec((tkn, width), new_blk)] * 2
    args += [kn, vn]
    if diff:
        in_specs += [pl.BlockSpec(lam.shape, const2), pl.BlockSpec(subln.shape, const2)]
        args += [lam, subln]
    else:
        if ncb:
            in_specs.append(pl.BlockSpec((1, tq, tkc), lambda b, s, qt, kt: (b, qt[s], jnp.minimum(kt[s], ncb - 1))))
            args.append(bias_c)
        in_specs.append(pl.BlockSpec((1, tq, tkn), lambda b, s, qt, kt: (b, qt[s], new_idx(b, s, qt, kt))))
        args.append(bias_n)
    tkmax = max(tkn, tkc)
    nbuf = min(heads * (rows // rc), max(2, 1024 // rc))
    scratch = ([pltpu.VMEM((heads, rows, LANES), BF16)] if diff else []) + (
        [pltpu.VMEM((2 * heads, tq, LANES // 2), BF16)] if diff and ncb else []) + [
        pltpu.VMEM((heads, rows, LANES), F32), pltpu.VMEM((heads, rows, LANES), F32),
        pltpu.VMEM((heads, rows, LANES), F32),
        pltpu.VMEM((nbuf, rc, tkmax), F32), pltpu.VMEM((nbuf, rc, tkmax), BF16),
        pltpu.VMEM((heads, tkmax, 2 * LANES), BF16), pltpu.VMEM((tq, tkmax), F32)]
    return pl.pallas_call(
        functools.partial(_attn_kernel, diff=diff, heads=heads, tq=tq, tkn=tkn, tkc=tkc, ncb=ncb,
                          past=past, t_valid=t, lam_init=lam_init, rc=rc),
        grid_spec=pltpu.PrefetchScalarGridSpec(
            num_scalar_prefetch=2, grid=(batch, len(steps)), in_specs=in_specs,
            out_specs=pl.BlockSpec((tq, width), qrow), scratch_shapes=scratch),
        out_shape=jax.ShapeDtypeStruct((n, width), BF16),
        compiler_params=_cparams(2),
        name="attn_diff" if diff else "attn_dsa",
    )(qt, kt, *args)


def _sort_key(score):
    bits = lax.bitcast_convert_type(score, I32)
    return jnp.where(bits < 0, bits ^ 0x7FFFFFFF, bits)


def _index_scores(iq_ref, iw_ref, kmat, idx_hd):
    sc = jnp.zeros((iq_ref.shape[0], kmat.shape[0]), F32)
    for h in range(IDX_HEADS):
        qh = iq_ref[:, h * idx_hd:(h + 1) * idx_hd].astype(BF16)
        w = iw_ref[:, idx_hd + h:idx_hd + h + 1]
        sc = sc + w * jnp.maximum(_dot_nt(qh, kmat), 0.0)
    return sc


def _topk_threshold(count_ge, tq, kf):
    def tbody(i, tu):
        cand = tu | jnp.left_shift(jnp.int32(1), jnp.int32(31) - i)
        return jnp.where(count_ge(cand ^ INT_MIN) >= kf, cand, tu)

    return lax.fori_loop(0, 32, tbody, jnp.zeros((tq, 1), I32)) ^ INT_MIN


def _tie_cutoff(count_eq_before, need, tq, nbits):
    def jbody(i, j):
        cand = j | jnp.left_shift(jnp.int32(1), jnp.int32(nbits - 1) - i)
        return jnp.where(count_eq_before(cand) < need, cand, j)

    return lax.fori_loop(0, nbits, jbody, jnp.zeros((tq, 1), I32))


def _bias_from_keys(key_ref, j_ref, w, *, tq, topk):
    kf = float(topk)
    count = lambda pred: jnp.sum(jnp.where(pred, 1.0, 0.0), axis=1, keepdims=True)
    thr = _topk_threshold(lambda cs: count(key_ref[:, :w] >= cs), tq, kf)
    key = key_ref[:, :w]
    eq = key == thr
    need = kf - count(key > thr)
    excess = (count(eq) > need) & (thr > KEY_NEG_INF)
    j_ref[...] = jnp.full((tq, 1), w, I32)
    idx = lax.broadcasted_iota(I32, (tq, w), 1)

    @pl.when(jnp.max(jnp.where(excess, 1.0, 0.0)) > 0.0)
    def _():
        j_ref[...] = _tie_cutoff(lambda cand: count((key_ref[:, :w] == thr) & (idx < cand)), need, tq,
                                 max(1, (w - 1).bit_length()))

    sel = ((key > thr) | (eq & (idx <= j_ref[...]))) & (key > KEY_NEG_INF) & (key < KEY_POS_INF)
    return jnp.where(sel, 0.0, NEG).astype(BF16)


def _select_kernel(*refs, tq, t, tn, past, topk, idx_hd, wstep):
    it = iter(refs)
    iq_ref, iw_ref, ikn_ref = next(it), next(it), next(it)
    ikc_ref = next(it) if past else None
    bc_ref = next(it) if past else None
    bn_ref, key_ref, j_ref = next(it), next(it), next(it)
    qi = pl.program_id(1)
    qpos = past + qi * tq + lax.broadcasted_iota(I32, (tq, 1), 0)

    def keys(kmat, kpos, valid):
        ok = (kpos >> CHUNK_SHIFT) <= (qpos >> CHUNK_SHIFT)
        if valid is not None:
            ok = ok & valid
        return jnp.where(ok, _sort_key(_index_scores(iq_ref, iw_ref, kmat, idx_hd)), KEY_NEG_INF)

    def run(wn):
        if past:
            key_ref[:, :past] = keys(ikc_ref[0, 0].astype(BF16), lax.broadcasted_iota(I32, (1, past), 1), None)
        c = lax.broadcasted_iota(I32, (1, wn), 1)
        key_ref[:, past:past + wn] = keys(ikn_ref[0, :wn, :], past + c, (c < t) if tn != t else None)
        bias = _bias_from_keys(key_ref, j_ref, past + wn, tq=tq, topk=topk)
        if past:
            bc_ref[0] = bias[:, :past]
        bn_ref[0, :, :wn] = bias[:, past:]
        if wn < tn:
            bn_ref[0, :, wn:] = jnp.full((tq, tn - wn), NEG, BF16)

    ncls = tn // wstep
    if ncls <= 1:
        run(tn)
    else:
        cls = ((qi + 1) * tq + wstep - 1) // wstep - 1
        for j in range(ncls):
            pl.when(cls == j)(functools.partial(run, (j + 1) * wstep))


def _select(iq, iw, ikn, *, batch, t, tq, past=0, ikc=None, layer=0):
    tn, idx_hd = ikn.shape[1], ikn.shape[2]
    nq = t // tq
    topk = min(TOPK_MAX, (past + t) // 4)
    wstep = SEL_WSTEP if (past == 0 and tn == t and tn % SEL_WSTEP == 0 and SEL_WSTEP % tq == 0) else tn
    row = lambda b, qi: (b * nq + qi, 0)
    in_specs = [pl.BlockSpec((tq, iq.shape[1]), row), pl.BlockSpec((tq, LANES), row),
                pl.BlockSpec((1, tn, idx_hd), lambda b, qi: (b, 0, 0))]
    args = [iq, iw, ikn]
    out_specs, out_shape = [], []
    if past:
        in_specs.append(pl.BlockSpec((1, 1, past, idx_hd), lambda b, qi: (layer, b, 0, 0)))
        args.append(ikc)
        out_specs.append(pl.BlockSpec((1, tq, past), lambda b, qi: (b, qi, 0)))
        out_shape.append(jax.ShapeDtypeStruct((batch, t, past), BF16))
    out_specs.append(pl.BlockSpec((1, tq, tn), lambda b, qi: (b, qi, 0)))
    out_shape.append(jax.ShapeDtypeStruct((batch, t, tn), BF16))
    out = pl.pallas_call(
        functools.partial(_select_kernel, tq=tq, t=t, tn=tn, past=past, topk=topk, idx_hd=idx_hd, wstep=wstep),
        grid=(batch, nq),
        in_specs=in_specs, out_specs=out_specs, out_shape=out_shape,
        scratch_shapes=[pltpu.VMEM((tq, past + tn), I32), pltpu.VMEM((tq, 1), I32)],
        compiler_params=_cparams(2),
        name="select",
    )(*args)
    return (out[0], out[1]) if past else (None, out[0])


def _post_kernel(x_ref, oa_ref, ob_ref, ga_ref, wa_ref, wb_ref, wg_ref, bg_ref, wo_ref, gf_ref,
                 xo_ref, h2_ref, mg_ref):
    d = x_ref.shape[1]
    x = x_ref[...]
    h = _rms(x, ga_ref[...]).astype(BF16)
    oa, ob = oa_ref[...], ob_ref[...]
    cw = 2 * LANES
    for c in range(0, d, cw):
        gate_a = jax.nn.sigmoid(_dot(h, wg_ref[:, c:c + cw]) + bg_ref[:, c:c + cw])
        gate_b = jax.nn.sigmoid(_dot(h, wg_ref[:, d + c:d + c + cw]) + bg_ref[:, d + c:d + c + cw])
        merged = gate_a * _dot(oa, wa_ref[:, c:c + cw]) + gate_b * _dot(ob, wb_ref[:, c:c + cw])
        mg_ref[:, c:c + cw] = merged.astype(BF16)
    mg = mg_ref[...]
    for c in range(0, d, cw):
        xo_ref[:, c:c + cw] = x[:, c:c + cw] + _dot(mg, wo_ref[:, c:c + cw])
    h2_ref[...] = _rms(xo_ref[...], gf_ref[...]).astype(BF16)


def _post(x, oa, ob, ga, wa, wb, wg, bg, wo, gf, *, tm):
    n, d = x.shape
    row = lambda i: (i, 0)
    const = lambda i: (0, 0)
    full = lambda a: pl.BlockSpec(a.shape, const)
    return pl.pallas_call(
        _post_kernel,
        grid=(n // tm,),
        in_specs=[pl.BlockSpec((tm, d), row)] * 3 + [full(a) for a in (ga, wa, wb, wg, bg, wo, gf)],
        out_specs=[pl.BlockSpec((tm, d), row)] * 2,
        out_shape=[jax.ShapeDtypeStruct((n, d), F32), jax.ShapeDtypeStruct((n, d), BF16)],
        scratch_shapes=[pltpu.VMEM((tm, d), BF16)],
        compiler_params=_cparams(1),
        name="post",
    )(x, oa, ob, ga, wa, wb, wg, bg, wo, gf)


def _ffn_kernel(x_ref, h_ref, wl_ref, wgt_ref, cwl_ref, cwg_ref, cbl_ref, cbg_ref, pl_ref, pg_ref,
                wd_ref, xo_ref, nl_ref, ng_ref, acc_ref, cl_ref, cg_ref, *, tm):
    i, j = pl.program_id(1), pl.program_id(2)
    nj = pl.num_programs(2)
    h = h_ref[...]
    rid = lax.broadcasted_iota(I32, (tm, 1), 0)

    def conv(w_ref, cw_ref, cb_ref, prev_ref, carry_ref, new_ref):
        u = _dot(h, w_ref[...])
        hal = jnp.where(i == 0, prev_ref[0], carry_ref[j, 6:8, :])
        u1 = jnp.where(rid == 0, hal[1:2], pltpu.roll(u, 1, 0))
        u2 = jnp.where(rid == 0, hal[0:1], jnp.where(rid == 1, hal[1:2], pltpu.roll(u, 2, 0)))
        cw = cw_ref[...]
        carry_ref[j] = u[tm - 8:, :]
        new_ref[0] = u[tm - 2:, :]
        return cb_ref[...] + u2 * cw[0:1] + u1 * cw[1:2] + u * cw[2:3]

    c_lin = conv(wl_ref, cwl_ref, cbl_ref, pl_ref, cl_ref, nl_ref)
    c_gate = conv(wgt_ref, cwg_ref, cbg_ref, pg_ref, cg_ref, ng_ref)
    act = (jax.nn.silu(c_gate) * c_lin).astype(BF16)
    part = _dot(act, wd_ref[...])

    @pl.when(j == 0)
    def _():
        acc_ref[...] = part

    @pl.when(j > 0)
    def _():
        acc_ref[...] += part

    @pl.when(j == nj - 1)
    def _():
        xo_ref[...] = x_ref[...] + acc_ref[...]


def _ffn(x, h2, w_up, conv_w, conv_b, conv_prev, w_down, *, batch, t, tm, tn):
    n, d = x.shape
    dff = w_down.shape[0]
    nt, nj = t // tm, dff // tn
    row = lambda b, i, j: (b * nt + i, 0)
    lin = lambda b, i, j: (0, j)
    gate = lambda b, i, j: (0, nj + j)
    tile_rows = lambda b, i, j: (b * nt + i, 0, j)
    return pl.pallas_call(
        functools.partial(_ffn_kernel, tm=tm),
        grid=(batch, nt, nj),
        in_specs=[pl.BlockSpec((tm, d), row), pl.BlockSpec((tm, d), row),
                  pl.BlockSpec((d, tn), lin), pl.BlockSpec((d, tn), gate),
                  pl.BlockSpec((CONV_W, tn), lin), pl.BlockSpec((CONV_W, tn), gate),
                  pl.BlockSpec((1, tn), lin), pl.BlockSpec((1, tn), gate),
                  pl.BlockSpec((1, CONV_W - 1, tn), lambda b, i, j: (b, 0, j)),
                  pl.BlockSpec((1, CONV_W - 1, tn), lambda b, i, j: (b, 0, nj + j)),
                  pl.BlockSpec((tn, d), lambda b, i, j: (j, 0))],
        out_specs=[pl.BlockSpec((tm, d), row),
                   pl.BlockSpec((1, CONV_W - 1, tn), tile_rows),
                   pl.BlockSpec((1, CONV_W - 1, tn), tile_rows)],
        out_shape=[jax.ShapeDtypeStruct((n, d), F32),
                   jax.ShapeDtypeStruct((batch * nt, CONV_W - 1, dff), F32),
                   jax.ShapeDtypeStruct((batch * nt, CONV_W - 1, dff), F32)],
        scratch_shapes=[pltpu.VMEM((tm, d), F32), pltpu.VMEM((nj, 8, tn), F32),
                        pltpu.VMEM((nj, 8, tn), F32)],
        compiler_params=_cparams(3),
        name="ffn",
    )(x, h2, w_up, w_up, conv_w, conv_w, conv_b, conv_b, conv_prev, conv_prev, w_down)


def _final_kernel(x_ref, g_ref, o_ref):
    o_ref[...] = _rms(x_ref[...], g_ref[...])


def _final_norm(x, g, *, tm):
    n, d = x.shape
    return pl.pallas_call(
        _final_kernel, grid=(n // tm,),
        in_specs=[pl.BlockSpec((tm, d), lambda i: (i, 0)), pl.BlockSpec((1, d), lambda i: (0, 0))],
        out_specs=pl.BlockSpec((tm, d), lambda i: (i, 0)),
        out_shape=jax.ShapeDtypeStruct((n, d), F32),
        compiler_params=_cparams(1), name="final_norm",
    )(x, g)


def _rope_tables(pos, tm):
    def ang(hd):
        half = hd // 2
        inv = ROPE_THETA ** (-(jnp.arange(half, dtype=F32) / half))
        return pos.astype(F32)[:, None] * inv[None, :]

    a64, a128 = ang(64), ang(128)
    cos64 = jnp.tile(jnp.cos(a64), (1, 4))
    sin64 = jnp.tile(jnp.sin(a64), (1, 4))
    lane = jnp.arange(LANES)[None, :]
    low64 = (lane % 64) < 32
    sa64 = jnp.where(low64, -sin64, 0.0)
    sb64 = jnp.where(low64, 0.0, sin64)
    cos128 = jnp.tile(jnp.cos(a128), (1, 2))
    sin128 = jnp.tile(jnp.sin(a128), (1, 2))
    s128 = jnp.where(lane < 64, -sin128, sin128)
    tabs = (cos64, sa64, sb64, cos128, s128)
    reps = -(-tm // pos.shape[0])
    return tuple(jnp.tile(tb, (reps, 1)) for tb in tabs)


def _ffn_chunk(dff):
    best = LANES
    for k in range(1, dff // LANES + 1):
        if dff % (k * LANES) == 0 and k * LANES <= 1536:
            best = k * LANES
    return best


def kernel(x_prompt, x_sample, cache_diff_k, cache_diff_v, cache_dsa_k, cache_dsa_v, cache_idx_k,
           state_ffn_conv, norm_attn, w_in, diff_lambda, diff_subln, w_branch_a, w_branch_b, w_gate,
           b_gate, w_out, norm_ffn, w_up, conv_w, conv_b, w_down, norm_final):
    depth, dec_batch, past = cache_diff_k.shape[:3]
    batch, seq, d_model = x_prompt.shape
    dec_seq = x_sample.shape[1]
    diff_heads2, diff_hd = cache_diff_k.shape[3:]
    dsa_heads, dsa_hd = cache_dsa_k.shape[3:]
    idx_hd = cache_idx_k.shape[3]
    dw, sw, iqw = diff_heads2 * diff_hd, dsa_heads * dsa_hd, IDX_HEADS * idx_hd
    dff = w_down.shape[1]
    assert diff_hd == 64 and idx_hd == 64 and dsa_hd == LANES and 2 * diff_hd == LANES
    assert w_in.shape[2] == 3 * dw + 3 * sw + iqw + idx_hd + IDX_HEADS
    assert seq % CHUNK == 0 and dec_seq % CHUNK == 0 and past % CHUNK == 0

    o = [0]
    for size in (dw, dw, dw, sw, sw, sw, iqw, idx_hd, IDX_HEADS):
        o.append(o[-1] + size)
    wb = w_in.astype(BF16)
    w64 = jnp.concatenate([wb[:, :, o[0]:o[2]], wb[:, :, o[6]:o[7]]], axis=2)
    w128 = wb[:, :, o[3]:o[5]]
    wv = jnp.concatenate([wb[:, :, o[2]:o[3]], wb[:, :, o[5]:o[6]]], axis=2)
    wikw = jnp.pad(wb[:, :, o[7]:o[9]], ((0, 0), (0, 0), (0, LANES - idx_hd - IDX_HEADS)))
    wa_b, wb_b, wg_b, wo_b = (w.astype(BF16) for w in (w_branch_a, w_branch_b, w_gate, w_out))
    wup_b, wdn_b = w_up.astype(BF16), w_down.astype(BF16)
    r2 = lambda a: a.reshape(a.shape[0], 1, a.shape[1])

    cfgs = []
    for name, x, bsz, t, pst in (("p", x_prompt, batch, seq, 0), ("s", x_sample, dec_batch, dec_seq, past)):
        n = bsz * t
        tm = _pick(n, 256)
        tq = _pick(t, 512)
        tkn = max(tq, LANES)
        pos = pst + jnp.arange(t, dtype=jnp.int32)
        cfgs.append(dict(name=name, x=x.reshape(n, d_model), bsz=bsz, t=t, past=pst, n=n, tm=tm, tq=tq,
                         tkn=tkn, tn_rows=max(t, tkn), tabs=_rope_tables(pos, tm), n_tab=max(t // tm, 1),
                         tqs=_pick(t, 512), tmf=_pick(t, 512), rows=[]))

    caches = None
    if past:
        flat = lambda a: a.reshape(depth, dec_batch, past * a.shape[3], a.shape[4])
        caches = dict(dk=flat(cache_diff_k), dv=flat(cache_diff_v), sk=flat(cache_dsa_k),
                      sv=flat(cache_dsa_v))
    tkc = _pick(past, 512) if past else 0
    tn_ffn = _ffn_chunk(dff)

    for l in range(depth):
        lam_init = 0.8 - 0.6 * math.exp(-0.3 * l)
        for cf in cfgs:
            bsz, t, pst, n = cf["bsz"], cf["t"], cf["past"], cf["n"]
            (dq, dkb, dkf, iq, sq, skb, skf, dvb, dvf, svb, svf, ikb, ikf, iw) = _proj(
                cf["x"], r2(norm_attn)[l], (w64[l], w128[l], wv[l], wikw[l]), cf["tabs"],
                tm=cf["tm"], n_tab=cf["n_tab"], dw=dw, sw=sw, iqw=iqw, idx_hd=idx_hd)

            def padrows(a):
                if cf["tn_rows"] == t:
                    return a
                a = a.reshape(bsz, t, a.shape[1])
                a = jnp.pad(a, ((0, 0), (0, cf["tn_rows"] - t), (0, 0)))
                return a.reshape(bsz * cf["tn_rows"], -1)

            has_cache = bool(pst)
            common = dict(batch=bsz, t=t, tq=cf["tq"], tkn=cf["tkn"], past=pst, layer=l, tkc=tkc)
            oa = _attn(dq, padrows(dkb), padrows(dvb), diff=True,
                       cache=(caches["dk"], caches["dv"]) if has_cache else None,
                       lam=diff_lambda[l], subln=r2(diff_subln)[l], lam_init=lam_init, **common)
            ikn = padrows(ikb).reshape(bsz, cf["tn_rows"], idx_hd)
            bias_c, bias_n = _select(iq, iw, ikn, batch=bsz, t=t, tq=cf["tqs"], past=pst,
                                     ikc=cache_idx_k if has_cache else None, layer=l)
            ob = _attn(sq, padrows(skb), padrows(svb), diff=False,
                       cache=(caches["sk"], caches["sv"]) if has_cache else None,
                       bias_c=bias_c, bias_n=bias_n, **common)
            x_mid, h2 = _post(cf["x"], oa, ob, r2(norm_attn)[l], wa_b[l], wb_b[l], wg_b[l], r2(b_gate)[l],
                              wo_b[l], r2(norm_ffn)[l], tm=cf["tm"])
            prev = state_ffn_conv[l] if has_cache else jnp.zeros((bsz, CONV_W - 1, 2 * dff), F32)
            x_new, new_l, new_g = _ffn(x_mid, h2, wup_b[l], conv_w[l], r2(conv_b)[l], prev, wdn_b[l],
                                       batch=bsz, t=t, tm=cf["tmf"], tn=tn_ffn)
            last_tile = lambda a: a.reshape(bsz, t // cf["tmf"], CONV_W - 1, dff)[:, -1]
            cf["x"] = x_new
            cf["rows"].append((dkf.reshape(bsz, t, diff_heads2, diff_hd),
                               dvf.reshape(bsz, t, diff_heads2 // 2, 2 * diff_hd),
                               skf.reshape(bsz, t, dsa_heads, dsa_hd),
                               svf.reshape(bsz, t, dsa_heads, dsa_hd),
                               ikf.reshape(bsz, t, idx_hd),
                               jnp.concatenate([last_tile(new_l), last_tile(new_g)], axis=-1)))

    outs = []
    for cf in cfgs:
        y = _final_norm(cf["x"], norm_final.reshape(1, -1), tm=cf["tm"])
        outs.append(y.reshape(cf["bsz"], cf["t"], d_model))
    stacks = [[jnp.stack(z, axis=0) for z in zip(*cf["rows"])] for cf in cfgs]
    return (outs[0], outs[1], *stacks[0], *stacks[1])
```

```python
import functools
import math

import numpy as np
import jax
import jax.numpy as jnp
from jax import lax
from jax.experimental import pallas as pl
from jax.experimental.pallas import tpu as pltpu

F32 = jnp.float32
BF16 = jnp.bfloat16
I32 = jnp.int32

CHUNK = 64
CHUNK_SHIFT = 6
IDX_HEADS = 4
TOPK_MAX = 256
CONV_W = 3
ROPE_THETA = 10000.0
EPS = 1e-6

LANES = 128
NEG = -1e30
LOG2E = 1.4426950408889634
INT_MIN = -(2 ** 31)
KEY_NEG_INF = INT_MIN + 0x7FFFFF
KEY_POS_INF = 0x7F800000
VMEM_LIMIT = 56 * 1024 * 1024
SEL_WSTEP = 512


def _cparams(n_axes):
    return pltpu.CompilerParams(dimension_semantics=("arbitrary",) * n_axes,
                                vmem_limit_bytes=VMEM_LIMIT)


def _rms(x, g):
    return x * lax.rsqrt(jnp.mean(x * x, axis=-1, keepdims=True) + EPS) * g


def _dot(a, b):
    return jnp.dot(a, b, preferred_element_type=F32)


def _dot_nt(a, b):
    return lax.dot_general(a, b, (((1,), (1,)), ((), ())), preferred_element_type=F32)


def _pick(n, pref):
    t = min(n, pref)
    while n % t:
        t //= 2
    return t


def _proj_kernel(x_ref, g_ref, w64_ref, w128_ref, wv_ref, wikw_ref,
                 c64_ref, sa64_ref, sb64_ref, c128_ref, s128_ref,
                 dq_ref, dkb_ref, dkf_ref, iq_ref, sq_ref, skb_ref, skf_ref,
                 dvb_ref, dvf_ref, svb_ref, svf_ref, ikb_ref, ikf_ref, iw_ref,
                 *, dw, sw, idx_hd):
    h = _rms(x_ref[...], g_ref[...]).astype(BF16)
    c64, sa64, sb64 = c64_ref[...], sa64_ref[...], sb64_ref[...]
    c128, s128 = c128_ref[...], s128_ref[...]

    def rope64(y):
        return y * c64 + pltpu.roll(y, LANES - 32, 1) * sa64 + pltpu.roll(y, 32, 1) * sb64

    def rope128(y):
        return y * c128 + pltpu.roll(y, 64, 1) * s128

    def slices(w_ref):
        n = w_ref.shape[1]
        for c0 in range(0, n, 2 * LANES):
            y = _dot(h, w_ref[:, c0:c0 + 2 * LANES])
            for c in (c0, c0 + LANES):
                yield c, y[:, c - c0:c - c0 + LANES]

    diff_scale = 64 ** -0.5 * LOG2E
    dsa_scale = LANES ** -0.5 * LOG2E
    idx_scale = 0.125

    for c, y in slices(w64_ref):
        r = rope64(y)
        if c < dw:
            dq_ref[:, c:c + LANES] = (r * diff_scale).astype(BF16)
        elif c < 2 * dw:
            dkf_ref[:, c - dw:c - dw + LANES] = r
            dkb_ref[:, c - dw:c - dw + LANES] = r.astype(BF16)
        else:
            iq_ref[:, c - 2 * dw:c - 2 * dw + LANES] = r * idx_scale
    for c, y in slices(w128_ref):
        r = rope128(y)
        if c < sw:
            sq_ref[:, c:c + LANES] = (r * dsa_scale).astype(BF16)
        else:
            skf_ref[:, c - sw:c - sw + LANES] = r
            skb_ref[:, c - sw:c - sw + LANES] = r.astype(BF16)
    for c, y in slices(wv_ref):
        if c < dw:
            dvf_ref[:, c:c + LANES] = y
            dvb_ref[:, c:c + LANES] = y.astype(BF16)
        else:
            svf_ref[:, c - dw:c - dw + LANES] = y
            svb_ref[:, c - dw:c - dw + LANES] = y.astype(BF16)
    y = _dot(h, wikw_ref[...])
    r = rope64(y)
    ikf_ref[...] = r[:, :idx_hd]
    ikb_ref[...] = r[:, :idx_hd].astype(BF16)
    iw_ref[...] = y * (IDX_HEADS ** -0.5)


def _proj(x, g, wts, tabs, *, tm, n_tab, dw, sw, iqw, idx_hd):
    n, d = x.shape
    w64, w128, wv, wikw = wts
    row = lambda i: (i, 0)
    const = lambda i: (0, 0)
    tab = lambda i: (i % n_tab, 0)

    def o(width, dtype):
        return jax.ShapeDtypeStruct((n, width), dtype), pl.BlockSpec((tm, width), row)

    outs = [o(dw, BF16), o(dw, BF16), o(dw, F32), o(iqw, F32),
            o(sw, BF16), o(sw, BF16), o(sw, F32),
            o(dw, BF16), o(dw, F32), o(sw, BF16), o(sw, F32),
            o(idx_hd, BF16), o(idx_hd, F32), o(LANES, F32)]
    return pl.pallas_call(
        functools.partial(_proj_kernel, dw=dw, sw=sw, idx_hd=idx_hd),
        grid=(n // tm,),
        in_specs=[pl.BlockSpec((tm, d), row), pl.BlockSpec((1, d), const)]
        + [pl.BlockSpec(w.shape, const) for w in (w64, w128, wv, wikw)]
        + [pl.BlockSpec((tm, LANES), tab)] * 5,
        out_specs=[s for _, s in outs],
        out_shape=[s for s, _ in outs],
        compiler_params=_cparams(1),
        name="proj",
    )(x, g, w64, w128, wv, wikw, *tabs)


def _attn_kernel(qt_ref, kt_ref, *refs, diff, heads, tq, tkn, tkc, ncb, past, t_valid, lam_init, rc):
    it = iter(refs)
    q_ref = next(it)
    kc_ref = vc_ref = bc_ref = bn_ref = lam_ref = sub_ref = qs_ref = None
    if ncb:
        kc_ref, vc_ref = next(it), next(it)
    kn_ref, vn_ref = next(it), next(it)
    if diff:
        lam_ref, sub_ref = next(it), next(it)
    else:
        if ncb:
            bc_ref = next(it)
        bn_ref = next(it)
    o_ref = next(it)
    if diff:
        qs_ref = next(it)
    m_ref, l_ref, acc_ref, s_ref, p_ref, vx_ref, bias_ref = (next(it) for _ in range(7))

    step_id = pl.program_id(1)
    qi, ki = qt_ref[step_id], kt_ref[step_id]
    rows = 2 * tq if diff else tq
    hs = lambda h: slice(h * LANES, (h + 1) * LANES)

    @pl.when(ki == 0)
    def _():
        m_ref[...] = jnp.full(m_ref.shape, NEG, F32)
        l_ref[...] = jnp.zeros(l_ref.shape, F32)
        acc_ref[...] = jnp.zeros(acc_ref.shape, F32)
        vx_ref[:, :, LANES:] = jnp.ones((heads, vx_ref.shape[1], LANES), BF16)
        if diff:
            lane = lax.broadcasted_iota(I32, (tq, LANES), 1)
            for h in range(heads):
                qh = q_ref[:, hs(h)]
                zero = jnp.zeros_like(qh)
                qs_ref[h, :tq, :] = jnp.where(lane < LANES // 2, qh, zero)
                qs_ref[h, tq:, :] = jnp.where(lane >= LANES // 2, qh, zero)

    brows = bias_ref.shape[0]

    def step(get_k, get_v, tk, biased, k_transposed=False):
        n = 0
        for h in range(heads):
            vx_ref[h, :tk, :LANES] = get_v(h)
            kh = get_k(h)
            for r0 in range(0, rows, rc):
                rs = slice(r0, r0 + rc)
                buf = n % s_ref.shape[0]
                n += 1
                qh = qs_ref[h, rs, :] if diff else q_ref[rs, hs(h)]
                s_ref[buf, :, :tk] = _dot(qh, kh) if k_transposed else _dot_nt(qh, kh)

                def col(c, buf=buf, r0=r0):
                    x = s_ref[buf, :, c * LANES:(c + 1) * LANES]
                    if biased:
                        b0 = r0 % brows
                        x = x + bias_ref[b0:b0 + rc, c * LANES:(c + 1) * LANES]
                    return x

                mx = col(0)
                for c in range(1, tk // LANES):
                    mx = jnp.maximum(mx, col(c))
                m_prev = m_ref[h, rs, :]
                m_new = jnp.maximum(m_prev, jnp.max(mx, axis=1, keepdims=True))
                alpha = jnp.exp2(m_prev - m_new)
                for c in range(tk // LANES):
                    p_ref[buf, :, c * LANES:(c + 1) * LANES] = jnp.exp2(col(c) - m_new).astype(BF16)
                pv = _dot(p_ref[buf, :, :tk], vx_ref[h, :tk, :])
                acc_ref[h, rs, :] = alpha * acc_ref[h, rs, :] + pv[:, :LANES]
                l_ref[h, rs, :] = alpha * l_ref[h, rs, :] + pv[:, LANES:]
                m_ref[h, rs, :] = m_new

    if ncb:
        @pl.when(ki < ncb)
        def _():
            if not diff:
                bias_ref[:, :tkc] = bc_ref[0].astype(F32)

            def cached(ref, i):
                n = ref.shape[2] // tkc
                return ref[0, 0, pl.ds(i, tkc, stride=n), :].astype(BF16)

            if diff:
                pair = lambda h: jnp.concatenate([kc_ref[0, 0, 2 * h], kc_ref[0, 0, 2 * h + 1]],
                                                 axis=0).astype(BF16)
                step(pair, functools.partial(cached, vc_ref), tkc, False, k_transposed=True)
            else:
                step(functools.partial(cached, kc_ref), functools.partial(cached, vc_ref), tkc, True)

    kj = ki - ncb
    get_kn, get_vn = (lambda h: kn_ref[:, hs(h)]), (lambda h: vn_ref[:, hs(h)])

    if diff:
        @pl.when((kj >= 0) & (kj < qi))
        def _():
            step(get_kn, get_vn, tkn, False)

        @pl.when(kj == qi)
        def _():
            r = lax.broadcasted_iota(I32, (tq, tkn), 0)
            c = lax.broadcasted_iota(I32, (tq, tkn), 1)
            ok = ((past + kj * tkn + c) >> CHUNK_SHIFT) <= ((past + qi * tq + r) >> CHUNK_SHIFT)
            if t_valid % tkn:
                ok = ok & (c < t_valid - kj * tkn)
            for rep in range(brows // tq):
                bias_ref[rep * tq:(rep + 1) * tq, :tkn] = jnp.where(ok, 0.0, NEG)
            step(get_kn, get_vn, tkn, True)
    else:
        @pl.when(kj >= 0)
        def _():
            bias_ref[:, :tkn] = bn_ref[0].astype(F32)
            step(get_kn, get_vn, tkn, True)

    @pl.when(kj == qi)
    def _():
        if diff:
            lf = lam_ref[...]
            lam = (jnp.exp(jnp.sum(lf[0:1] * lf[1:2], axis=1, keepdims=True))
                   - jnp.exp(jnp.sum(lf[2:3] * lf[3:4], axis=1, keepdims=True)) + lam_init)
            g = sub_ref[...]
        for h in range(heads):
            o = acc_ref[h] / l_ref[h]
            if diff:
                o = o[:tq] - lam * o[tq:]
                o = _rms(o, g) * (1.0 - lam_init)
            o_ref[:, hs(h)] = o.astype(BF16)


def _attn(q, kn, vn, *, diff, batch, t, tq, tkn, past=0, cache=None, layer=0, tkc=0,
          bias_c=None, bias_n=None, lam=None, subln=None, lam_init=0.0):
    n, width = q.shape
    heads = width // LANES
    nq = t // tq
    ncb = 0 if cache is None else past // tkc
    nkn = kn.shape[0] // batch // tkn
    rows = 2 * tq if diff else tq
    rc = min(rows, 256)
    assert rows % rc == 0 and (rc % tq == 0 or tq % rc == 0)
    steps = [(qi, ki) for qi in range(nq) for ki in range(ncb + qi + 1)]
    qt = jnp.asarray(np.array([s[0] for s in steps], np.int32))
    kt = jnp.asarray(np.array([s[1] for s in steps], np.int32))

    def new_idx(b, s, qt, kt):
        return jnp.maximum(kt[s] - ncb, 0)

    qrow = lambda b, s, qt, kt: (b * nq + qt[s], 0)
    new_blk = lambda b, s, qt, kt: (b * nkn + new_idx(b, s, qt, kt), 0)
    const2 = lambda b, s, qt, kt: (0, 0)
    in_specs = [pl.BlockSpec((tq, width), qrow)]
    args = [q]
    if ncb:
        cblk = lambda b, s, qt, kt: jnp.minimum(kt[s], ncb - 1)
        for c in cache:
            if c.ndim == 5:
                in_specs.append(pl.BlockSpec((1, 1, c.shape[2], c.shape[3], tkc),
                                             lambda b, s, qt, kt: (layer, b, 0, 0, cblk(b, s, qt, kt))))
            else:
                in_specs.append(pl.BlockSpec((1, 1, c.shape[2] // ncb, c.shape[3]),
                                             lambda b, s, qt, kt: (layer, b, cblk(b, s, qt, kt), 0)))
        args += list(cache)
    in_specs += [pl.BlockSpec((tkn, width), new_blk)] * 2
    args += [kn, vn]
    if diff:
        in_specs += [pl.BlockSpec(lam.shape, const2), pl.BlockSpec(subln.shape, const2)]
        args += [lam, subln]
    else:
        if ncb:
            in_specs.append(pl.BlockSpec((1, tq, tkc), lambda b, s, qt, kt: (b, qt[s], jnp.minimum(kt[s], ncb - 1))))
            args.append(bias_c)
        in_specs.append(pl.BlockSpec((1, tq, tkn), lambda b, s, qt, kt: (b, qt[s], new_idx(b, s, qt, kt))))
        args.append(bias_n)
    tkmax = max(tkn, tkc)
    nbuf = min(heads * (rows // rc), max(2, 1024 // rc))
    scratch = ([pltpu.VMEM((heads, rows, LANES), BF16)] if diff else []) + [
        pltpu.VMEM((heads, rows, LANES), F32), pltpu.VMEM((heads, rows, LANES), F32),
        pltpu.VMEM((heads, rows, LANES), F32),
        pltpu.VMEM((nbuf, rc, tkmax), F32), pltpu.VMEM((nbuf, rc, tkmax), BF16),
        pltpu.VMEM((heads, tkmax, 2 * LANES), BF16), pltpu.VMEM((max(tq, rc), tkmax), F32)]
    return pl.pallas_call(
        functools.partial(_attn_kernel, diff=diff, heads=heads, tq=tq, tkn=tkn, tkc=tkc, ncb=ncb,
                          past=past, t_valid=t, lam_init=lam_init, rc=rc),
        grid_spec=pltpu.PrefetchScalarGridSpec(
            num_scalar_prefetch=2, grid=(batch, len(steps)), in_specs=in_specs,
            out_specs=pl.BlockSpec((tq, width), qrow), scratch_shapes=scratch),
        out_shape=jax.ShapeDtypeStruct((n, width), BF16),
        compiler_params=_cparams(2),
        name="attn_diff" if diff else "attn_dsa",
    )(qt, kt, *args)


def _sort_key(score):
    bits = lax.bitcast_convert_type(score, I32)
    return jnp.where(bits < 0, bits ^ 0x7FFFFFFF, bits)


def _index_scores(iq_ref, iw_ref, kmat, idx_hd, k_transposed=False):
    sc = jnp.zeros((iq_ref.shape[0], kmat.shape[1 if k_transposed else 0]), F32)
    for h in range(IDX_HEADS):
        qh = iq_ref[:, h * idx_hd:(h + 1) * idx_hd].astype(BF16)
        w = iw_ref[:, idx_hd + h:idx_hd + h + 1]
        sc = sc + w * jnp.maximum(_dot(qh, kmat) if k_transposed else _dot_nt(qh, kmat), 0.0)
    return sc


def _topk_threshold(count_ge, tq, kf):
    def tbody(i, tu):
        cand = tu | jnp.left_shift(jnp.int32(1), jnp.int32(31) - i)
        return jnp.where(count_ge(cand ^ INT_MIN) >= kf, cand, tu)

    return lax.fori_loop(0, 32, tbody, jnp.zeros((tq, 1), I32)) ^ INT_MIN


def _tie_cutoff(count_eq_before, need, tq, nbits):
    def jbody(i, j):
        cand = j | jnp.left_shift(jnp.int32(1), jnp.int32(nbits - 1) - i)
        return jnp.where(count_eq_before(cand) < need, cand, j)

    return lax.fori_loop(0, nbits, jbody, jnp.zeros((tq, 1), I32))


def _bias_from_keys(key_ref, j_ref, w, *, tq, topk):
    kf = float(topk)
    count = lambda pred: jnp.sum(jnp.where(pred, 1.0, 0.0), axis=1, keepdims=True)
    thr = _topk_threshold(lambda cs: count(key_ref[:, :w] >= cs), tq, kf)
    key = key_ref[:, :w]
    eq = key == thr
    need = kf - count(key > thr)
    excess = (count(eq) > need) & (thr > KEY_NEG_INF)
    j_ref[...] = jnp.full((tq, 1), w, I32)
    idx = lax.broadcasted_iota(I32, (tq, w), 1)

    @pl.when(jnp.max(jnp.where(excess, 1.0, 0.0)) > 0.0)
    def _():
        j_ref[...] = _tie_cutoff(lambda cand: count((key_ref[:, :w] == thr) & (idx < cand)), need, tq,
                                 max(1, (w - 1).bit_length()))

    sel = ((key > thr) | (eq & (idx <= j_ref[...]))) & (key > KEY_NEG_INF) & (key < KEY_POS_INF)
    return jnp.where(sel, 0.0, NEG).astype(BF16)


def _select_kernel(*refs, tq, t, tn, past, topk, idx_hd, wstep):
    it = iter(refs)
    iq_ref, iw_ref, ikn_ref = next(it), next(it), next(it)
    ikc_ref = next(it) if past else None
    bc_ref = next(it) if past else None
    bn_ref, key_ref, j_ref = next(it), next(it), next(it)
    qi = pl.program_id(1)
    qpos = past + qi * tq + lax.broadcasted_iota(I32, (tq, 1), 0)

    def keys(kmat, kpos, valid, k_transposed=False):
        ok = (kpos >> CHUNK_SHIFT) <= (qpos >> CHUNK_SHIFT)
        if valid is not None:
            ok = ok & valid
        sc = _index_scores(iq_ref, iw_ref, kmat, idx_hd, k_transposed)
        return jnp.where(ok, _sort_key(sc), KEY_NEG_INF)

    def run(wn):
        if past:
            key_ref[:, :past] = keys(ikc_ref[0, 0].astype(BF16), lax.broadcasted_iota(I32, (1, past), 1), None,
                                     k_transposed=True)
        c = lax.broadcasted_iota(I32, (1, wn), 1)
        key_ref[:, past:past + wn] = keys(ikn_ref[0, :wn, :], past + c, (c < t) if tn != t else None)
        bias = _bias_from_keys(key_ref, j_ref, past + wn, tq=tq, topk=topk)
        if past:
            bc_ref[0] = bias[:, :past]
        bn_ref[0, :, :wn] = bias[:, past:]
        if wn < tn:
            bn_ref[0, :, wn:] = jnp.full((tq, tn - wn), NEG, BF16)

    ncls = tn // wstep
    if ncls <= 1:
        run(tn)
    else:
        cls = ((qi + 1) * tq + wstep - 1) // wstep - 1
        for j in range(ncls):
            pl.when(cls == j)(functools.partial(run, (j + 1) * wstep))


def _select(iq, iw, ikn, *, batch, t, tq, past=0, ikc=None, layer=0):
    tn, idx_hd = ikn.shape[1], ikn.shape[2]
    nq = t // tq
    topk = min(TOPK_MAX, (past + t) // 4)
    wstep = SEL_WSTEP if (past == 0 and tn == t and tn % SEL_WSTEP == 0 and SEL_WSTEP % tq == 0) else tn
    row = lambda b, qi: (b * nq + qi, 0)
    in_specs = [pl.BlockSpec((tq, iq.shape[1]), row), pl.BlockSpec((tq, LANES), row),
                pl.BlockSpec((1, tn, idx_hd), lambda b, qi: (b, 0, 0))]
    args = [iq, iw, ikn]
    out_specs, out_shape = [], []
    if past:
        in_specs.append(pl.BlockSpec((1, 1, idx_hd, past), lambda b, qi: (layer, b, 0, 0)))
        args.append(ikc)
        out_specs.append(pl.BlockSpec((1, tq, past), lambda b, qi: (b, qi, 0)))
        out_shape.append(jax.ShapeDtypeStruct((batch, t, past), BF16))
    out_specs.append(pl.BlockSpec((1, tq, tn), lambda b, qi: (b, qi, 0)))
    out_shape.append(jax.ShapeDtypeStruct((batch, t, tn), BF16))
    out = pl.pallas_call(
        functools.partial(_select_kernel, tq=tq, t=t, tn=tn, past=past, topk=topk, idx_hd=idx_hd, wstep=wstep),
        grid=(batch, nq),
        in_specs=in_specs, out_specs=out_specs, out_shape=out_shape,
        scratch_shapes=[pltpu.VMEM((tq, past + tn), I32), pltpu.VMEM((tq, 1), I32)],
        compiler_params=_cparams(2),
        name="select",
    )(*args)
    return (out[0], out[1]) if past else (None, out[0])


def _post_kernel(x_ref, oa_ref, ob_ref, ga_ref, wa_ref, wb_ref, wg_ref, bg_ref, wo_ref, gf_ref,
                 xo_ref, h2_ref, mg_ref):
    d = x_ref.shape[1]
    x = x_ref[...]
    h = _rms(x, ga_ref[...]).astype(BF16)
    oa, ob = oa_ref[...], ob_ref[...]
    cw = 2 * LANES
    for c in range(0, d, cw):
        gate_a = jax.nn.sigmoid(_dot(h, wg_ref[:, c:c + cw]) + bg_ref[:, c:c + cw])
        gate_b = jax.nn.sigmoid(_dot(h, wg_ref[:, d + c:d + c + cw]) + bg_ref[:, d + c:d + c + cw])
        merged = gate_a * _dot(oa, wa_ref[:, c:c + cw]) + gate_b * _dot(ob, wb_ref[:, c:c + cw])
        mg_ref[:, c:c + cw] = merged.astype(BF16)
    mg = mg_ref[...]
    for c in range(0, d, cw):
        xo_ref[:, c:c + cw] = x[:, c:c + cw] + _dot(mg, wo_ref[:, c:c + cw])
    h2_ref[...] = _rms(xo_ref[...], gf_ref[...]).astype(BF16)


def _post(x, oa, ob, ga, wa, wb, wg, bg, wo, gf, *, tm):
    n, d = x.shape
    row = lambda i: (i, 0)
    const = lambda i: (0, 0)
    full = lambda a: pl.BlockSpec(a.shape, const)
    return pl.pallas_call(
        _post_kernel,
        grid=(n // tm,),
        in_specs=[pl.BlockSpec((tm, d), row)] * 3 + [full(a) for a in (ga, wa, wb, wg, bg, wo, gf)],
        out_specs=[pl.BlockSpec((tm, d), row)] * 2,
        out_shape=[jax.ShapeDtypeStruct((n, d), F32), jax.ShapeDtypeStruct((n, d), BF16)],
        scratch_shapes=[pltpu.VMEM((tm, d), BF16)],
        compiler_params=_cparams(1),
        name="post",
    )(x, oa, ob, ga, wa, wb, wg, bg, wo, gf)


def _ffn_kernel(x_ref, h_ref, wl_ref, wgt_ref, cwl_ref, cwg_ref, cbl_ref, cbg_ref, pl_ref, pg_ref,
                wd_ref, xo_ref, nl_ref, ng_ref, acc_ref, cl_ref, cg_ref, *, tm):
    i, j = pl.program_id(1), pl.program_id(2)
    nj = pl.num_programs(2)
    h = h_ref[...]
    rid = lax.broadcasted_iota(I32, (tm, 1), 0)

    def conv(w_ref, cw_ref, cb_ref, prev_ref, carry_ref, new_ref):
        u = _dot(h, w_ref[...])
        hal = jnp.where(i == 0, prev_ref[0], carry_ref[j, 6:8, :])
        u1 = jnp.where(rid == 0, hal[1:2], pltpu.roll(u, 1, 0))
        u2 = jnp.where(rid == 0, hal[0:1], jnp.where(rid == 1, hal[1:2], pltpu.roll(u, 2, 0)))
        cw = cw_ref[...]
        carry_ref[j] = u[tm - 8:, :]
        new_ref[0] = u[tm - 2:, :]
        return cb_ref[...] + u2 * cw[0:1] + u1 * cw[1:2] + u * cw[2:3]

    c_lin = conv(wl_ref, cwl_ref, cbl_ref, pl_ref, cl_ref, nl_ref)
    c_gate = conv(wgt_ref, cwg_ref, cbg_ref, pg_ref, cg_ref, ng_ref)
    act = (jax.nn.silu(c_gate) * c_lin).astype(BF16)
    part = _dot(act, wd_ref[...])

    @pl.when(j == 0)
    def _():
        acc_ref[...] = part

    @pl.when(j > 0)
    def _():
        acc_ref[...] += part

    @pl.when(j == nj - 1)
    def _():
        xo_ref[...] = x_ref[...] + acc_ref[...]


def _ffn(x, h2, w_up, conv_w, conv_b, conv_prev, w_down, *, batch, t, tm, tn):
    n, d = x.shape
    dff = w_down.shape[0]
    nt, nj = t // tm, dff // tn
    row = lambda b, i, j: (b * nt + i, 0)
    lin = lambda b, i, j: (0, j)
    gate = lambda b, i, j: (0, nj + j)
    tile_rows = lambda b, i, j: (b * nt + i, 0, j)
    return pl.pallas_call(
        functools.partial(_ffn_kernel, tm=tm),
        grid=(batch, nt, nj),
        in_specs=[pl.BlockSpec((tm, d), row), pl.BlockSpec((tm, d), row),
                  pl.BlockSpec((d, tn), lin), pl.BlockSpec((d, tn), gate),
                  pl.BlockSpec((CONV_W, tn), lin), pl.BlockSpec((CONV_W, tn), gate),
                  pl.BlockSpec((1, tn), lin), pl.BlockSpec((1, tn), gate),
                  pl.BlockSpec((1, CONV_W - 1, tn), lambda b, i, j: (b, 0, j)),
                  pl.BlockSpec((1, CONV_W - 1, tn), lambda b, i, j: (b, 0, nj + j)),
                  pl.BlockSpec((tn, d), lambda b, i, j: (j, 0))],
        out_specs=[pl.BlockSpec((tm, d), row),
                   pl.BlockSpec((1, CONV_W - 1, tn), tile_rows),
                   pl.BlockSpec((1, CONV_W - 1, tn), tile_rows)],
        out_shape=[jax.ShapeDtypeStruct((n, d), F32),
                   jax.ShapeDtypeStruct((batch * nt, CONV_W - 1, dff), F32),
                   jax.ShapeDtypeStruct((batch * nt, CONV_W - 1, dff), F32)],
        scratch_shapes=[pltpu.VMEM((tm, d), F32), pltpu.VMEM((nj, 8, tn), F32),
                        pltpu.VMEM((nj, 8, tn), F32)],
        compiler_params=_cparams(3),
        name="ffn",
    )(x, h2, w_up, w_up, conv_w, conv_w, conv_b, conv_b, conv_prev, conv_prev, w_down)


def _final_kernel(x_ref, g_ref, o_ref):
    o_ref[...] = _rms(x_ref[...], g_ref[...])


def _final_norm(x, g, *, tm):
    n, d = x.shape
    return pl.pallas_call(
        _final_kernel, grid=(n // tm,),
        in_specs=[pl.BlockSpec((tm, d), lambda i: (i, 0)), pl.BlockSpec((1, d), lambda i: (0, 0))],
        out_specs=pl.BlockSpec((tm, d), lambda i: (i, 0)),
        out_shape=jax.ShapeDtypeStruct((n, d), F32),
        compiler_params=_cparams(1), name="final_norm",
    )(x, g)


def _rope_tables(pos, tm):
    def ang(hd):
        half = hd // 2
        inv = ROPE_THETA ** (-(jnp.arange(half, dtype=F32) / half))
        return pos.astype(F32)[:, None] * inv[None, :]

    a64, a128 = ang(64), ang(128)
    cos64 = jnp.tile(jnp.cos(a64), (1, 4))
    sin64 = jnp.tile(jnp.sin(a64), (1, 4))
    lane = jnp.arange(LANES)[None, :]
    low64 = (lane % 64) < 32
    sa64 = jnp.where(low64, -sin64, 0.0)
    sb64 = jnp.where(low64, 0.0, sin64)
    cos128 = jnp.tile(jnp.cos(a128), (1, 2))
    sin128 = jnp.tile(jnp.sin(a128), (1, 2))
    s128 = jnp.where(lane < 64, -sin128, sin128)
    tabs = (cos64, sa64, sb64, cos128, s128)
    reps = -(-tm // pos.shape[0])
    return tuple(jnp.tile(tb, (reps, 1)) for tb in tabs)


def _ffn_chunk(dff):
    best = LANES
    for k in range(1, dff // LANES + 1):
        if dff % (k * LANES) == 0 and k * LANES <= 1536:
            best = k * LANES
    return best


def kernel(x_prompt, x_sample, cache_diff_k, cache_diff_v, cache_dsa_k, cache_dsa_v, cache_idx_k,
           state_ffn_conv, norm_attn, w_in, diff_lambda, diff_subln, w_branch_a, w_branch_b, w_gate,
           b_gate, w_out, norm_ffn, w_up, conv_w, conv_b, w_down, norm_final):
    depth, dec_batch, past = cache_diff_k.shape[:3]
    batch, seq, d_model = x_prompt.shape
    dec_seq = x_sample.shape[1]
    diff_heads2, diff_hd = cache_diff_k.shape[3:]
    dsa_heads, dsa_hd = cache_dsa_k.shape[3:]
    idx_hd = cache_idx_k.shape[3]
    dw, sw, iqw = diff_heads2 * diff_hd, dsa_heads * dsa_hd, IDX_HEADS * idx_hd
    dff = w_down.shape[1]
    assert diff_hd == 64 and idx_hd == 64 and dsa_hd == LANES and 2 * diff_hd == LANES
    assert w_in.shape[2] == 3 * dw + 3 * sw + iqw + idx_hd + IDX_HEADS
    assert seq % CHUNK == 0 and dec_seq % CHUNK == 0 and past % CHUNK == 0

    o = [0]
    for size in (dw, dw, dw, sw, sw, sw, iqw, idx_hd, IDX_HEADS):
        o.append(o[-1] + size)
    wb = w_in.astype(BF16)
    w64 = jnp.concatenate([wb[:, :, o[0]:o[2]], wb[:, :, o[6]:o[7]]], axis=2)
    w128 = wb[:, :, o[3]:o[5]]
    wv = jnp.concatenate([wb[:, :, o[2]:o[3]], wb[:, :, o[5]:o[6]]], axis=2)
    wikw = jnp.pad(wb[:, :, o[7]:o[9]], ((0, 0), (0, 0), (0, LANES - idx_hd - IDX_HEADS)))
    wa_b, wb_b, wg_b, wo_b = (w.astype(BF16) for w in (w_branch_a, w_branch_b, w_gate, w_out))
    wup_b, wdn_b = w_up.astype(BF16), w_down.astype(BF16)
    r2 = lambda a: a.reshape(a.shape[0], 1, a.shape[1])

    cfgs = []
    for name, x, bsz, t, pst in (("p", x_prompt, batch, seq, 0), ("s", x_sample, dec_batch, dec_seq, past)):
        n = bsz * t
        tm = _pick(n, 256)
        tq = _pick(t, 512)
        tkn = max(tq, LANES)
        pos = pst + jnp.arange(t, dtype=jnp.int32)
        cfgs.append(dict(name=name, x=x.reshape(n, d_model), bsz=bsz, t=t, past=pst, n=n, tm=tm, tq=tq,
                         tkn=tkn, tn_rows=max(t, tkn), tabs=_rope_tables(pos, tm), n_tab=max(t // tm, 1),
                         tqs=_pick(t, 512), tmf=_pick(t, 512), rows=[]))

    caches = None
    if past:
        flat = lambda a: a.reshape(depth, dec_batch, past * a.shape[3], a.shape[4])
        caches = dict(dk=jnp.transpose(cache_diff_k, (0, 1, 3, 4, 2)), dv=flat(cache_diff_v),
                      sk=flat(cache_dsa_k), sv=flat(cache_dsa_v),
                      ik=jnp.transpose(cache_idx_k, (0, 1, 3, 2)))
    tkc = _pick(past, 512) if past else 0
    tn_ffn = _ffn_chunk(dff)

    for l in range(depth):
        lam_init = 0.8 - 0.6 * math.exp(-0.3 * l)
        for cf in cfgs:
            bsz, t, pst, n = cf["bsz"], cf["t"], cf["past"], cf["n"]
            (dq, dkb, dkf, iq, sq, skb, skf, dvb, dvf, svb, svf, ikb, ikf, iw) = _proj(
                cf["x"], r2(norm_attn)[l], (w64[l], w128[l], wv[l], wikw[l]), cf["tabs"],
                tm=cf["tm"], n_tab=cf["n_tab"], dw=dw, sw=sw, iqw=iqw, idx_hd=idx_hd)

            def padrows(a):
                if cf["tn_rows"] == t:
                    return a
                a = a.reshape(bsz, t, a.shape[1])
                a = jnp.pad(a, ((0, 0), (0, cf["tn_rows"] - t), (0, 0)))
                return a.reshape(bsz * cf["tn_rows"], -1)

            has_cache = bool(pst)
            common = dict(batch=bsz, t=t, tq=cf["tq"], tkn=cf["tkn"], past=pst, layer=l, tkc=tkc)
            oa = _attn(dq, padrows(dkb), padrows(dvb), diff=True,
                       cache=(caches["dk"], caches["dv"]) if has_cache else None,
                       lam=diff_lambda[l], subln=r2(diff_subln)[l], lam_init=lam_init, **common)
            ikn = padrows(ikb).reshape(bsz, cf["tn_rows"], idx_hd)
            bias_c, bias_n = _select(iq, iw, ikn, batch=bsz, t=t, tq=cf["tqs"], past=pst,
                                     ikc=caches["ik"] if has_cache else None, layer=l)
            ob = _attn(sq, padrows(skb), padrows(svb), diff=False,
                       cache=(caches["sk"], caches["sv"]) if has_cache else None,
                       bias_c=bias_c, bias_n=bias_n, **common)
            x_mid, h2 = _post(cf["x"], oa, ob, r2(norm_attn)[l], wa_b[l], wb_b[l], wg_b[l], r2(b_gate)[l],
                              wo_b[l], r2(norm_ffn)[l], tm=cf["tm"])
            prev = state_ffn_conv[l] if has_cache else jnp.zeros((bsz, CONV_W - 1, 2 * dff), F32)
            x_new, new_l, new_g = _ffn(x_mid, h2, wup_b[l], conv_w[l], r2(conv_b)[l], prev, wdn_b[l],
                                       batch=bsz, t=t, tm=cf["tmf"], tn=tn_ffn)
            last_tile = lambda a: a.reshape(bsz, t // cf["tmf"], CONV_W - 1, dff)[:, -1]
            cf["x"] = x_new
            cf["rows"].append((dkf.reshape(bsz, t, diff_heads2, diff_hd),
                               dvf.reshape(bsz, t, diff_heads2 // 2, 2 * diff_hd),
                               skf.reshape(bsz, t, dsa_heads, dsa_hd),
                               svf.reshape(bsz, t, dsa_heads, dsa_hd),
                               ikf.reshape(bsz, t, idx_hd),
                               jnp.concatenate([last_tile(new_l), last_tile(new_g)], axis=-1)))

    outs = []
    for cf in cfgs:
        y = _final_norm(cf["x"], norm_final.reshape(1, -1), tm=cf["tm"])
        outs.append(y.reshape(cf["bsz"], cf["t"], d_model))
    stacks = [[jnp.stack(z, axis=0) for z in zip(*cf["rows"])] for cf in cfgs]
    return (outs[0], outs[1], *stacks[0], *stacks[1])
```

```python
import functools
import math

import numpy as np
import jax
import jax.numpy as jnp
from jax import lax
from jax.experimental import pallas as pl
from jax.experimental.pallas import tpu as pltpu

F32 = jnp.float32
BF16 = jnp.bfloat16
I32 = jnp.int32

CHUNK = 64
CHUNK_SHIFT = 6
IDX_HEADS = 4
TOPK_MAX = 256
CONV_W = 3
ROPE_THETA = 10000.0
EPS = 1e-6

LANES = 128
NEG = -1e30
LOG2E = 1.4426950408889634
INT_MIN = -(2 ** 31)
KEY_NEG_INF = INT_MIN + 0x7FFFFF
KEY_POS_INF = 0x7F800000
VMEM_LIMIT = 56 * 1024 * 1024
SEL_WSTEP = 512


def _cparams(n_axes):
    return pltpu.CompilerParams(dimension_semantics=("arbitrary",) * n_axes,
                                vmem_limit_bytes=VMEM_LIMIT)


def _rms(x, g):
    return x * lax.rsqrt(jnp.mean(x * x, axis=-1, keepdims=True) + EPS) * g


def _dot(a, b):
    return jnp.dot(a, b, preferred_element_type=F32)


def _dot_nt(a, b):
    return lax.dot_general(a, b, (((1,), (1,)), ((), ())), preferred_element_type=F32)


def _pick(n, pref):
    t = min(n, pref)
    while n % t:
        t //= 2
    return t


def _proj_kernel(*refs, dw, sw, idx_hd, stacked):
    (x_ref, g_ref, w64_ref, w128_ref, wv_ref, wikw_ref,
     c64_ref, sa64_ref, sb64_ref, c128_ref, s128_ref) = refs[:11]
    (dq_ref, dkb_ref, dkf_ref, iq_ref, sq_ref, skb_ref, skf_ref,
     dvb_ref, dvf_ref, svb_ref, svf_ref, ikb_ref, ikf_ref, iw_ref) = refs[-14:]
    tm = x_ref.shape[0]
    half = LANES // 2

    def put64(ref, c, r):
        if stacked:
            rt = r.T
            ref[0, 0, c // half] = rt[:half]
            ref[0, 0, c // half + 1] = rt[half:]
        else:
            ref[:, c:c + LANES] = r

    def put128(ref, c, y, width):
        if stacked:
            ref[0, 0, pl.ds(c // LANES, tm, stride=width // LANES), :] = y
        else:
            ref[:, c:c + LANES] = y

    h = _rms(x_ref[...], g_ref[...]).astype(BF16)
    c64, sa64, sb64 = c64_ref[...], sa64_ref[...], sb64_ref[...]
    c128, s128 = c128_ref[...], s128_ref[...]

    def rope64(y):
        return y * c64 + pltpu.roll(y, LANES - 32, 1) * sa64 + pltpu.roll(y, 32, 1) * sb64

    def rope128(y):
        return y * c128 + pltpu.roll(y, 64, 1) * s128

    def slices(w_ref):
        n = w_ref.shape[1]
        for c0 in range(0, n, 2 * LANES):
            y = _dot(h, w_ref[:, c0:c0 + 2 * LANES])
            for c in (c0, c0 + LANES):
                yield c, y[:, c - c0:c - c0 + LANES]

    diff_scale = 64 ** -0.5 * LOG2E
    dsa_scale = LANES ** -0.5 * LOG2E
    idx_scale = 0.125

    for c, y in slices(w64_ref):
        r = rope64(y)
        if c < dw:
            dq_ref[:, c:c + LANES] = (r * diff_scale).astype(BF16)
        elif c < 2 * dw:
            put64(dkf_ref, c - dw, r)
            dkb_ref[:, c - dw:c - dw + LANES] = r.astype(BF16)
        else:
            iq_ref[:, c - 2 * dw:c - 2 * dw + LANES] = r * idx_scale
    for c, y in slices(w128_ref):
        r = rope128(y)
        if c < sw:
            sq_ref[:, c:c + LANES] = (r * dsa_scale).astype(BF16)
        else:
            put128(skf_ref, c - sw, r, sw)
            skb_ref[:, c - sw:c - sw + LANES] = r.astype(BF16)
    for c, y in slices(wv_ref):
        if c < dw:
            put128(dvf_ref, c, y, dw)
            dvb_ref[:, c:c + LANES] = y.astype(BF16)
        else:
            put128(svf_ref, c - dw, y, sw)
            svb_ref[:, c - dw:c - dw + LANES] = y.astype(BF16)
    y = _dot(h, wikw_ref[...])
    r = rope64(y)
    if stacked:
        ikf_ref[0, 0] = r.T[:idx_hd]
    else:
        ikf_ref[...] = r[:, :idx_hd]
    ikb_ref[...] = r[:, :idx_hd].astype(BF16)
    iw_ref[...] = y * (IDX_HEADS ** -0.5)


_STACKED_OUTS = (2, 6, 8, 10, 12)


def _proj(x, g, wts, tabs, *, tm, n_tab, dw, sw, iqw, idx_hd, stack=None):
    n, d = x.shape
    w64, w128, wv, wikw = wts
    row = lambda i: (i, 0)
    const = lambda i: (0, 0)
    tab = lambda i: (i % n_tab, 0)

    def o(width, dtype):
        return jax.ShapeDtypeStruct((n, width), dtype), pl.BlockSpec((tm, width), row)

    outs = [o(dw, BF16), o(dw, BF16), o(dw, F32), o(iqw, F32),
            o(sw, BF16), o(sw, BF16), o(sw, F32),
            o(dw, BF16), o(dw, F32), o(sw, BF16), o(sw, F32),
            o(idx_hd, BF16), o(idx_hd, F32), o(LANES, F32)]
    extra_in, extra_specs, aliases = [], [], {}
    if stack is not None:
        depth, layer, batch, t, prev = stack
        nt = t // tm
        half = LANES // 2

        def wide(width):
            hn = width // LANES
            return (jax.ShapeDtypeStruct((depth, batch, t * hn, LANES), F32),
                    pl.BlockSpec((1, 1, tm * hn, LANES), lambda i: (layer, i // nt, i % nt, 0)))

        outs[2] = (jax.ShapeDtypeStruct((depth, batch, dw // half, half, t), F32),
                   pl.BlockSpec((1, 1, dw // half, half, tm), lambda i: (layer, i // nt, 0, 0, i % nt)))
        outs[6], outs[8], outs[10] = wide(sw), wide(dw), wide(sw)
        outs[12] = (jax.ShapeDtypeStruct((depth, batch, idx_hd, t), F32),
                    pl.BlockSpec((1, 1, idx_hd, tm), lambda i: (layer, i // nt, 0, i % nt)))
        if prev is not None:
            extra_in = list(prev)
            extra_specs = [pl.BlockSpec(memory_space=pl.ANY)] * len(prev)
            aliases = {11 + k: o_idx for k, o_idx in enumerate(_STACKED_OUTS)}
    return pl.pallas_call(
        functools.partial(_proj_kernel, dw=dw, sw=sw, idx_hd=idx_hd, stacked=stack is not None),
        grid=(n // tm,),
        in_specs=[pl.BlockSpec((tm, d), row), pl.BlockSpec((1, d), const)]
        + [pl.BlockSpec(w.shape, const) for w in (w64, w128, wv, wikw)]
        + [pl.BlockSpec((tm, LANES), tab)] * 5 + extra_specs,
        out_specs=[s for _, s in outs],
        out_shape=[s for s, _ in outs],
        input_output_aliases=aliases,
        compiler_params=_cparams(1),
        name="proj",
    )(x, g, w64, w128, wv, wikw, *tabs, *extra_in)


def _attn_kernel(qt_ref, kt_ref, *refs, diff, heads, tq, tkn, tkc, ncb, past, t_valid, lam_init, rc):
    it = iter(refs)
    q_ref = next(it)
    kc_ref = vc_ref = bc_ref = bn_ref = lam_ref = sub_ref = qs_ref = None
    if ncb:
        kc_ref, vc_ref = next(it), next(it)
    kn_ref, vn_ref = next(it), next(it)
    if diff:
        lam_ref, sub_ref = next(it), next(it)
    else:
        if ncb:
            bc_ref = next(it)
        bn_ref = next(it)
    o_ref = next(it)
    if diff:
        qs_ref = next(it)
    m_ref, l_ref, acc_ref, s_ref, p_ref, vx_ref, bias_ref = (next(it) for _ in range(7))

    step_id = pl.program_id(1)
    qi, ki = qt_ref[step_id], kt_ref[step_id]
    rows = 2 * tq if diff else tq
    hs = lambda h: slice(h * LANES, (h + 1) * LANES)

    @pl.when(ki == 0)
    def _():
        m_ref[...] = jnp.full(m_ref.shape, NEG, F32)
        l_ref[...] = jnp.zeros(l_ref.shape, F32)
        acc_ref[...] = jnp.zeros(acc_ref.shape, F32)
        vx_ref[:, :, LANES:] = jnp.ones((heads, vx_ref.shape[1], LANES), BF16)
        if diff:
            lane = lax.broadcasted_iota(I32, (tq, LANES), 1)
            for h in range(heads):
                qh = q_ref[:, hs(h)]
                zero = jnp.zeros_like(qh)
                qs_ref[h, :tq, :] = jnp.where(lane < LANES // 2, qh, zero)
                qs_ref[h, tq:, :] = jnp.where(lane >= LANES // 2, qh, zero)

    brows = bias_ref.shape[0]

    def step(get_k, get_v, tk, biased, k_transposed=False, diag=False):
        n = 0
        for h in range(heads):
            vx_ref[h, :tk, :LANES] = get_v(h)
            kh = get_k(h)
            for r0 in range(0, rows, rc):
                rs = slice(r0, r0 + rc)
                buf = n % s_ref.shape[0]
                n += 1
                qh = qs_ref[h, rs, :] if diff else q_ref[rs, hs(h)]
                tc = min(tk, r0 % tq + rc) if (diag and tk == tq and rc <= tq) else tk
                s_ref[buf, :, :tc] = _dot(qh, kh[:, :tc]) if k_transposed else _dot_nt(qh, kh[:tc])

                def col(c, buf=buf, r0=r0):
                    x = s_ref[buf, :, c * LANES:(c + 1) * LANES]
                    if biased:
                        b0 = r0 % brows
                        x = x + bias_ref[b0:b0 + rc, c * LANES:(c + 1) * LANES]
                    return x

                mx = col(0)
                for c in range(1, tc // LANES):
                    mx = jnp.maximum(mx, col(c))
                m_prev = m_ref[h, rs, :]
                m_new = jnp.maximum(m_prev, jnp.max(mx, axis=1, keepdims=True))
                alpha = jnp.exp2(m_prev - m_new)
                for c in range(tc // LANES):
                    p_ref[buf, :, c * LANES:(c + 1) * LANES] = jnp.exp2(col(c) - m_new).astype(BF16)
                pv = _dot(p_ref[buf, :, :tc], vx_ref[h, :tc, :])
                acc_ref[h, rs, :] = alpha * acc_ref[h, rs, :] + pv[:, :LANES]
                l_ref[h, rs, :] = alpha * l_ref[h, rs, :] + pv[:, LANES:]
                m_ref[h, rs, :] = m_new

    if ncb:
        @pl.when(ki < ncb)
        def _():
            if not diff:
                bias_ref[:, :tkc] = bc_ref[0].astype(F32)

            def cached(ref, i):
                n = ref.shape[2] // tkc
                return ref[0, 0, pl.ds(i, tkc, stride=n), :].astype(BF16)

            if diff:
                pair = lambda h: jnp.concatenate([kc_ref[0, 0, 2 * h], kc_ref[0, 0, 2 * h + 1]],
                                                 axis=0).astype(BF16)
                step(pair, functools.partial(cached, vc_ref), tkc, False, k_transposed=True)
            else:
                step(functools.partial(cached, kc_ref), functools.partial(cached, vc_ref), tkc, True)

    kj = ki - ncb
    get_kn, get_vn = (lambda h: kn_ref[:, hs(h)]), (lambda h: vn_ref[:, hs(h)])

    if diff:
        @pl.when((kj >= 0) & (kj < qi))
        def _():
            step(get_kn, get_vn, tkn, False)

        @pl.when(kj == qi)
        def _():
            r = lax.broadcasted_iota(I32, (tq, tkn), 0)
            c = lax.broadcasted_iota(I32, (tq, tkn), 1)
            ok = ((past + kj * tkn + c) >> CHUNK_SHIFT) <= ((past + qi * tq + r) >> CHUNK_SHIFT)
            if t_valid % tkn:
                ok = ok & (c < t_valid - kj * tkn)
            for rep in range(brows // tq):
                bias_ref[rep * tq:(rep + 1) * tq, :tkn] = jnp.where(ok, 0.0, NEG)
            step(get_kn, get_vn, tkn, True, diag=True)
    else:
        @pl.when((kj >= 0) & (kj < qi))
        def _():
            bias_ref[:, :tkn] = bn_ref[0].astype(F32)
            step(get_kn, get_vn, tkn, True)

        @pl.when(kj == qi)
        def _():
            bias_ref[:, :tkn] = bn_ref[0].astype(F32)
            step(get_kn, get_vn, tkn, True, diag=True)

    @pl.when(kj == qi)
    def _():
        if diff:
            lf = lam_ref[...]
            lam = (jnp.exp(jnp.sum(lf[0:1] * lf[1:2], axis=1, keepdims=True))
                   - jnp.exp(jnp.sum(lf[2:3] * lf[3:4], axis=1, keepdims=True)) + lam_init)
            g = sub_ref[...]
        for h in range(heads):
            o = acc_ref[h] / l_ref[h]
            if diff:
                o = o[:tq] - lam * o[tq:]
                o = _rms(o, g) * (1.0 - lam_init)
            o_ref[:, hs(h)] = o.astype(BF16)


def _attn(q, kn, vn, *, diff, batch, t, tq, tkn, past=0, cache=None, layer=0, tkc=0,
          bias_c=None, bias_n=None, lam=None, subln=None, lam_init=0.0):
    n, width = q.shape
    heads = width // LANES
    nq = t // tq
    ncb = 0 if cache is None else past // tkc
    nkn = kn.shape[0] // batch // tkn
    rows = 2 * tq if diff else tq
    rc = min(rows, 256)
    assert rows % rc == 0 and (rc % tq == 0 or tq % rc == 0)
    steps = [(qi, ki) for qi in range(nq) for ki in range(ncb + qi + 1)]
    qt = jnp.asarray(np.array([s[0] for s in steps], np.int32))
    kt = jnp.asarray(np.array([s[1] for s in steps], np.int32))

    def new_idx(b, s, qt, kt):
        return jnp.maximum(kt[s] - ncb, 0)

    qrow = lambda b, s, qt, kt: (b * nq + qt[s], 0)
    new_blk = lambda b, s, qt, kt: (b * nkn + new_idx(b, s, qt, kt), 0)
    const2 = lambda b, s, qt, kt: (0, 0)
    in_specs = [pl.BlockSpec((tq, width), qrow)]
    args = [q]
    if ncb:
        cblk = lambda b, s, qt, kt: jnp.minimum(kt[s], ncb - 1)
        for c in cache:
            if c.ndim == 5:
                in_specs.append(pl.BlockSpec((1, 1, c.shape[2], c.shape[3], tkc),
                                             lambda b, s, qt, kt: (layer, b, 0, 0, cblk(b, s, qt, kt))))
            else:
                in_specs.append(pl.BlockSpec((1, 1, c.shape[2] // ncb, c.shape[3]),
                                             lambda b, s, qt, kt: (layer, b, cblk(b, s, qt, kt), 0)))
        args += list(cache)
    in_specs += [pl.BlockSpec((tkn, width), new_blk)] * 2
    args += [kn, vn]
    if diff:
        in_specs += [pl.BlockSpec(lam.shape, const2), pl.BlockSpec(subln.shape, const2)]
        args += [lam, subln]
    else:
        if ncb:
            in_specs.append(pl.BlockSpec((1, tq, tkc), lambda b, s, qt, kt: (b, qt[s], jnp.minimum(kt[s], ncb - 1))))
            args.append(bias_c)
        in_specs.append(pl.BlockSpec((1, tq, tkn), lambda b, s, qt, kt: (b, qt[s], new_idx(b, s, qt, kt))))
        args.append(bias_n)
    tkmax = max(tkn, tkc)
    nbuf = min(heads * (rows // rc), max(2, 1024 // rc))
    scratch = ([pltpu.VMEM((heads, rows, LANES), BF16)] if diff else []) + [
        pltpu.VMEM((heads, rows, LANES), F32), pltpu.VMEM((heads, rows, LANES), F32),
        pltpu.VMEM((heads, rows, LANES), F32),
        pltpu.VMEM((nbuf, rc, tkmax), F32), pltpu.VMEM((nbuf, rc, tkmax), BF16),
        pltpu.VMEM((heads, tkmax, 2 * LANES), BF16), pltpu.VMEM((max(tq, rc), tkmax), F32)]
    return pl.pallas_call(
        functools.partial(_attn_kernel, diff=diff, heads=heads, tq=tq, tkn=tkn, tkc=tkc, ncb=ncb,
                          past=past, t_valid=t, lam_init=lam_init, rc=rc),
        grid_spec=pltpu.PrefetchScalarGridSpec(
            num_scalar_prefetch=2, grid=(batch, len(steps)), in_specs=in_specs,
            out_specs=pl.BlockSpec((tq, width), qrow), scratch_shapes=scratch),
        out_shape=jax.ShapeDtypeStruct((n, width), BF16),
        compiler_params=_cparams(2),
        name="attn_diff" if diff else "attn_dsa",
    )(qt, kt, *args)


def _sort_key(score):
    bits = lax.bitcast_convert_type(score, I32)
    return jnp.where(bits < 0, bits ^ 0x7FFFFFFF, bits)


def _index_scores(iq_ref, iw_ref, kmat, idx_hd, k_transposed=False):
    sc = jnp.zeros((iq_ref.shape[0], kmat.shape[1 if k_transposed else 0]), F32)
    for h in range(IDX_HEADS):
        qh = iq_ref[:, h * idx_hd:(h + 1) * idx_hd].astype(BF16)
        w = iw_ref[:, idx_hd + h:idx_hd + h + 1]
        sc = sc + w * jnp.maximum(_dot(qh, kmat) if k_transposed else _dot_nt(qh, kmat), 0.0)
    return sc


def _topk_threshold(count_ge, tq, kf):
    def tbody(i, tu):
        cand = tu | jnp.left_shift(jnp.int32(1), jnp.int32(31) - i)
        return jnp.where(count_ge(cand ^ INT_MIN) >= kf, cand, tu)

    return lax.fori_loop(0, 32, tbody, jnp.zeros((tq, 1), I32)) ^ INT_MIN


def _tie_cutoff(count_eq_before, need, tq, nbits):
    def jbody(i, j):
        cand = j | jnp.left_shift(jnp.int32(1), jnp.int32(nbits - 1) - i)
        return jnp.where(count_eq_before(cand) < need, cand, j)

    return lax.fori_loop(0, nbits, jbody, jnp.zeros((tq, 1), I32))


def _bias_from_keys(key_ref, j_ref, w, *, tq, topk):
    kf = float(topk)
    count = lambda pred: jnp.sum(jnp.where(pred, 1.0, 0.0), axis=1, keepdims=True)
    thr = _topk_threshold(lambda cs: count(key_ref[:, :w] >= cs), tq, kf)
    key = key_ref[:, :w]
    eq = key == thr
    need = kf - count(key > thr)
    excess = (count(eq) > need) & (thr > KEY_NEG_INF)
    j_ref[...] = jnp.full((tq, 1), w, I32)
    idx = lax.broadcasted_iota(I32, (tq, w), 1)

    @pl.when(jnp.max(jnp.where(excess, 1.0, 0.0)) > 0.0)
    def _():
        j_ref[...] = _tie_cutoff(lambda cand: count((key_ref[:, :w] == thr) & (idx < cand)), need, tq,
                                 max(1, (w - 1).bit_length()))

    sel = ((key > thr) | (eq & (idx <= j_ref[...]))) & (key > KEY_NEG_INF) & (key < KEY_POS_INF)
    return jnp.where(sel, 0.0, NEG).astype(BF16)


def _select_kernel(*refs, tq, t, tn, past, topk, idx_hd, wstep):
    it = iter(refs)
    iq_ref, iw_ref, ikn_ref = next(it), next(it), next(it)
    ikc_ref = next(it) if past else None
    bc_ref = next(it) if past else None
    bn_ref, key_ref, j_ref = next(it), next(it), next(it)
    qi = pl.program_id(1)
    qpos = past + qi * tq + lax.broadcasted_iota(I32, (tq, 1), 0)

    def keys(kmat, kpos, valid, k_transposed=False):
        ok = (kpos >> CHUNK_SHIFT) <= (qpos >> CHUNK_SHIFT)
        if valid is not None:
            ok = ok & valid
        sc = _index_scores(iq_ref, iw_ref, kmat, idx_hd, k_transposed)
        return jnp.where(ok, _sort_key(sc), KEY_NEG_INF)

    def run(wn):
        if past:
            key_ref[:, :past] = keys(ikc_ref[0, 0].astype(BF16), lax.broadcasted_iota(I32, (1, past), 1), None,
                                     k_transposed=True)
        c = lax.broadcasted_iota(I32, (1, wn), 1)
        key_ref[:, past:past + wn] = keys(ikn_ref[0, :wn, :], past + c, (c < t) if tn != t else None)
        bias = _bias_from_keys(key_ref, j_ref, past + wn, tq=tq, topk=topk)
        if past:
            bc_ref[0] = bias[:, :past]
        bn_ref[0, :, :wn] = bias[:, past:]
        if wn < tn:
            bn_ref[0, :, wn:] = jnp.full((tq, tn - wn), NEG, BF16)

    ncls = tn // wstep
    if ncls <= 1:
        run(tn)
    else:
        cls = ((qi + 1) * tq + wstep - 1) // wstep - 1
        for j in range(ncls):
            pl.when(cls == j)(functools.partial(run, (j + 1) * wstep))


def _select(iq, iw, ikn, *, batch, t, tq, past=0, ikc=None, layer=0):
    tn, idx_hd = ikn.shape[1], ikn.shape[2]
    nq = t // tq
    topk = min(TOPK_MAX, (past + t) // 4)
    wstep = SEL_WSTEP if (past == 0 and tn == t and tn % SEL_WSTEP == 0 and SEL_WSTEP % tq == 0) else tn
    row = lambda b, qi: (b * nq + qi, 0)
    in_specs = [pl.BlockSpec((tq, iq.shape[1]), row), pl.BlockSpec((tq, LANES), row),
                pl.BlockSpec((1, tn, idx_hd), lambda b, qi: (b, 0, 0))]
    args = [iq, iw, ikn]
    out_specs, out_shape = [], []
    if past:
        in_specs.append(pl.BlockSpec((1, 1, idx_hd, past), lambda b, qi: (layer, b, 0, 0)))
        args.append(ikc)
        out_specs.append(pl.BlockSpec((1, tq, past), lambda b, qi: (b, qi, 0)))
        out_shape.append(jax.ShapeDtypeStruct((batch, t, past), BF16))
    out_specs.append(pl.BlockSpec((1, tq, tn), lambda b, qi: (b, qi, 0)))
    out_shape.append(jax.ShapeDtypeStruct((batch, t, tn), BF16))
    out = pl.pallas_call(
        functools.partial(_select_kernel, tq=tq, t=t, tn=tn, past=past, topk=topk, idx_hd=idx_hd, wstep=wstep),
        grid=(batch, nq),
        in_specs=in_specs, out_specs=out_specs, out_shape=out_shape,
        scratch_shapes=[pltpu.VMEM((tq, past + tn), I32), pltpu.VMEM((tq, 1), I32)],
        compiler_params=_cparams(2),
        name="select",
    )(*args)
    return (out[0], out[1]) if past else (None, out[0])


def _post_kernel(x_ref, oa_ref, ob_ref, ga_ref, wa_ref, wb_ref, wg_ref, bg_ref, wo_ref, gf_ref,
                 xo_ref, h2_ref, mg_ref):
    d = x_ref.shape[1]
    x = x_ref[...]
    h = _rms(x, ga_ref[...]).astype(BF16)
    oa, ob = oa_ref[...], ob_ref[...]
    cw = 2 * LANES
    for c in range(0, d, cw):
        gate_a = jax.nn.sigmoid(_dot(h, wg_ref[:, c:c + cw]) + bg_ref[:, c:c + cw])
        gate_b = jax.nn.sigmoid(_dot(h, wg_ref[:, d + c:d + c + cw]) + bg_ref[:, d + c:d + c + cw])
        merged = gate_a * _dot(oa, wa_ref[:, c:c + cw]) + gate_b * _dot(ob, wb_ref[:, c:c + cw])
        mg_ref[:, c:c + cw] = merged.astype(BF16)
    mg = mg_ref[...]
    for c in range(0, d, cw):
        xo_ref[:, c:c + cw] = x[:, c:c + cw] + _dot(mg, wo_ref[:, c:c + cw])
    h2_ref[...] = _rms(xo_ref[...], gf_ref[...]).astype(BF16)


def _post(x, oa, ob, ga, wa, wb, wg, bg, wo, gf, *, tm):
    n, d = x.shape
    row = lambda i: (i, 0)
    const = lambda i: (0, 0)
    full = lambda a: pl.BlockSpec(a.shape, const)
    return pl.pallas_call(
        _post_kernel,
        grid=(n // tm,),
        in_specs=[pl.BlockSpec((tm, d), row)] * 3 + [full(a) for a in (ga, wa, wb, wg, bg, wo, gf)],
        out_specs=[pl.BlockSpec((tm, d), row)] * 2,
        out_shape=[jax.ShapeDtypeStruct((n, d), F32), jax.ShapeDtypeStruct((n, d), BF16)],
        scratch_shapes=[pltpu.VMEM((tm, d), BF16)],
        compiler_params=_cparams(1),
        name="post",
    )(x, oa, ob, ga, wa, wb, wg, bg, wo, gf)


def _ffn_kernel(x_ref, h_ref, wl_ref, wgt_ref, cwl_ref, cwg_ref, cbl_ref, cbg_ref, pl_ref, pg_ref,
                wd_ref, xo_ref, nl_ref, ng_ref, acc_ref, cl_ref, cg_ref, *, tm):
    i, j = pl.program_id(1), pl.program_id(2)
    nj = pl.num_programs(2)
    h = h_ref[...]
    rid = lax.broadcasted_iota(I32, (tm, 1), 0)

    def conv(w_ref, cw_ref, cb_ref, prev_ref, carry_ref, new_ref):
        u = _dot(h, w_ref[...])
        hal = jnp.where(i == 0, prev_ref[0], carry_ref[j, 6:8, :])
        u1 = jnp.where(rid == 0, hal[1:2], pltpu.roll(u, 1, 0))
        u2 = jnp.where(rid == 0, hal[0:1], jnp.where(rid == 1, hal[1:2], pltpu.roll(u, 2, 0)))
        cw = cw_ref[...]
        carry_ref[j] = u[tm - 8:, :]
        new_ref[0] = u[tm - 2:, :]
        return cb_ref[...] + u2 * cw[0:1] + u1 * cw[1:2] + u * cw[2:3]

    c_lin = conv(wl_ref, cwl_ref, cbl_ref, pl_ref, cl_ref, nl_ref)
    c_gate = conv(wgt_ref, cwg_ref, cbg_ref, pg_ref, cg_ref, ng_ref)
    act = (jax.nn.silu(c_gate) * c_lin).astype(BF16)
    part = _dot(act, wd_ref[...])

    @pl.when(j == 0)
    def _():
        acc_ref[...] = part

    @pl.when(j > 0)
    def _():
        acc_ref[...] += part

    @pl.when(j == nj - 1)
    def _():
        xo_ref[...] = x_ref[...] + acc_ref[...]


def _ffn(x, h2, w_up, conv_w, conv_b, conv_prev, w_down, *, batch, t, tm, tn):
    n, d = x.shape
    dff = w_down.shape[0]
    nt, nj = t // tm, dff // tn
    row = lambda b, i, j: (b * nt + i, 0)
    lin = lambda b, i, j: (0, j)
    gate = lambda b, i, j: (0, nj + j)
    tile_rows = lambda b, i, j: (b * nt + i, 0, j)
    return pl.pallas_call(
        functools.partial(_ffn_kernel, tm=tm),
        grid=(batch, nt, nj),
        in_specs=[pl.BlockSpec((tm, d), row), pl.BlockSpec((tm, d), row),
                  pl.BlockSpec((d, tn), lin), pl.BlockSpec((d, tn), gate),
                  pl.BlockSpec((CONV_W, tn), lin), pl.BlockSpec((CONV_W, tn), gate),
                  pl.BlockSpec((1, tn), lin), pl.BlockSpec((1, tn), gate),
                  pl.BlockSpec((1, CONV_W - 1, tn), lambda b, i, j: (b, 0, j)),
                  pl.BlockSpec((1, CONV_W - 1, tn), lambda b, i, j: (b, 0, nj + j)),
                  pl.BlockSpec((tn, d), lambda b, i, j: (j, 0))],
        out_specs=[pl.BlockSpec((tm, d), row),
                   pl.BlockSpec((1, CONV_W - 1, tn), tile_rows),
                   pl.BlockSpec((1, CONV_W - 1, tn), tile_rows)],
        out_shape=[jax.ShapeDtypeStruct((n, d), F32),
                   jax.ShapeDtypeStruct((batch * nt, CONV_W - 1, dff), F32),
                   jax.ShapeDtypeStruct((batch * nt, CONV_W - 1, dff), F32)],
        scratch_shapes=[pltpu.VMEM((tm, d), F32), pltpu.VMEM((nj, 8, tn), F32),
                        pltpu.VMEM((nj, 8, tn), F32)],
        compiler_params=_cparams(3),
        name="ffn",
    )(x, h2, w_up, w_up, conv_w, conv_w, conv_b, conv_b, conv_prev, conv_prev, w_down)


def _final_kernel(x_ref, g_ref, o_ref):
    o_ref[...] = _rms(x_ref[...], g_ref[...])


def _final_norm(x, g, *, tm):
    n, d = x.shape
    return pl.pallas_call(
        _final_kernel, grid=(n // tm,),
        in_specs=[pl.BlockSpec((tm, d), lambda i: (i, 0)), pl.BlockSpec((1, d), lambda i: (0, 0))],
        out_specs=pl.BlockSpec((tm, d), lambda i: (i, 0)),
        out_shape=jax.ShapeDtypeStruct((n, d), F32),
        compiler_params=_cparams(1), name="final_norm",
    )(x, g)


def _rope_tables(pos, tm):
    def ang(hd):
        half = hd // 2
        inv = ROPE_THETA ** (-(jnp.arange(half, dtype=F32) / half))
        return pos.astype(F32)[:, None] * inv[None, :]

    a64, a128 = ang(64), ang(128)
    cos64 = jnp.tile(jnp.cos(a64), (1, 4))
    sin64 = jnp.tile(jnp.sin(a64), (1, 4))
    lane = jnp.arange(LANES)[None, :]
    low64 = (lane % 64) < 32
    sa64 = jnp.where(low64, -sin64, 0.0)
    sb64 = jnp.where(low64, 0.0, sin64)
    cos128 = jnp.tile(jnp.cos(a128), (1, 2))
    sin128 = jnp.tile(jnp.sin(a128), (1, 2))
    s128 = jnp.where(lane < 64, -sin128, sin128)
    tabs = (cos64, sa64, sb64, cos128, s128)
    reps = -(-tm // pos.shape[0])
    return tuple(jnp.tile(tb, (reps, 1)) for tb in tabs)


def _ffn_chunk(dff):
    best = LANES
    for k in range(1, dff // LANES + 1):
        if dff % (k * LANES) == 0 and k * LANES <= 1536:
            best = k * LANES
    return best


def kernel(x_prompt, x_sample, cache_diff_k, cache_diff_v, cache_dsa_k, cache_dsa_v, cache_idx_k,
           state_ffn_conv, norm_attn, w_in, diff_lambda, diff_subln, w_branch_a, w_branch_b, w_gate,
           b_gate, w_out, norm_ffn, w_up, conv_w, conv_b, w_down, norm_final):
    depth, dec_batch, past = cache_diff_k.shape[:3]
    batch, seq, d_model = x_prompt.shape
    dec_seq = x_sample.shape[1]
    diff_heads2, diff_hd = cache_diff_k.shape[3:]
    dsa_heads, dsa_hd = cache_dsa_k.shape[3:]
    idx_hd = cache_idx_k.shape[3]
    dw, sw, iqw = diff_heads2 * diff_hd, dsa_heads * dsa_hd, IDX_HEADS * idx_hd
    dff = w_down.shape[1]
    assert diff_hd == 64 and idx_hd == 64 and dsa_hd == LANES and 2 * diff_hd == LANES
    assert w_in.shape[2] == 3 * dw + 3 * sw + iqw + idx_hd + IDX_HEADS
    assert seq % CHUNK == 0 and dec_seq % CHUNK == 0 and past % CHUNK == 0

    o = [0]
    for size in (dw, dw, dw, sw, sw, sw, iqw, idx_hd, IDX_HEADS):
        o.append(o[-1] + size)
    wb = w_in.astype(BF16)
    w64 = jnp.concatenate([wb[:, :, o[0]:o[2]], wb[:, :, o[6]:o[7]]], axis=2)
    w128 = wb[:, :, o[3]:o[5]]
    wv = jnp.concatenate([wb[:, :, o[2]:o[3]], wb[:, :, o[5]:o[6]]], axis=2)
    wikw = jnp.pad(wb[:, :, o[7]:o[9]], ((0, 0), (0, 0), (0, LANES - idx_hd - IDX_HEADS)))
    wa_b, wb_b, wg_b, wo_b = (w.astype(BF16) for w in (w_branch_a, w_branch_b, w_gate, w_out))
    wup_b, wdn_b = w_up.astype(BF16), w_down.astype(BF16)
    r2 = lambda a: a.reshape(a.shape[0], 1, a.shape[1])

    cfgs = []
    for name, x, bsz, t, pst in (("p", x_prompt, batch, seq, 0), ("s", x_sample, dec_batch, dec_seq, past)):
        n = bsz * t
        tm = _pick(n, 256)
        tq = _pick(t, 512)
        tkn = max(tq, LANES)
        pos = pst + jnp.arange(t, dtype=jnp.int32)
        cfgs.append(dict(name=name, x=x.reshape(n, d_model), bsz=bsz, t=t, past=pst, n=n, tm=tm, tq=tq,
                         tkn=tkn, tn_rows=max(t, tkn), tabs=_rope_tables(pos, tm), n_tab=max(t // tm, 1),
                         tqs=_pick(t, 512), tmf=_pick(t, 512), rows=[]))

    caches = None
    if past:
        flat = lambda a: a.reshape(depth, dec_batch, past * a.shape[3], a.shape[4])
        caches = dict(dk=jnp.transpose(cache_diff_k, (0, 1, 3, 4, 2)), dv=flat(cache_diff_v),
                      sk=flat(cache_dsa_k), sv=flat(cache_dsa_v),
                      ik=jnp.transpose(cache_idx_k, (0, 1, 3, 2)))
    tkc = _pick(past, 512) if past else 0
    tn_ffn = _ffn_chunk(dff)

    for l in range(depth):
        lam_init = 0.8 - 0.6 * math.exp(-0.3 * l)
        for cf in cfgs:
            bsz, t, pst, n = cf["bsz"], cf["t"], cf["past"], cf["n"]
            stacked = t % cf["tm"] == 0 and cf["tm"] % LANES == 0
            (dq, dkb, dkf, iq, sq, skb, skf, dvb, dvf, svb, svf, ikb, ikf, iw) = _proj(
                cf["x"], r2(norm_attn)[l], (w64[l], w128[l], wv[l], wikw[l]), cf["tabs"],
                tm=cf["tm"], n_tab=cf["n_tab"], dw=dw, sw=sw, iqw=iqw, idx_hd=idx_hd,
                stack=(depth, l, bsz, t, cf.get("stk")) if stacked else None)
            if stacked:
                cf["stk"] = (dkf, skf, dvf, svf, ikf)

            def padrows(a):
                if cf["tn_rows"] == t:
                    return a
                a = a.reshape(bsz, t, a.shape[1])
                a = jnp.pad(a, ((0, 0), (0, cf["tn_rows"] - t), (0, 0)))
                return a.reshape(bsz * cf["tn_rows"], -1)

            has_cache = bool(pst)
            common = dict(batch=bsz, t=t, tq=cf["tq"], tkn=cf["tkn"], past=pst, layer=l, tkc=tkc)
            oa = _attn(dq, padrows(dkb), padrows(dvb), diff=True,
                       cache=(caches["dk"], caches["dv"]) if has_cache else None,
                       lam=diff_lambda[l], subln=r2(diff_subln)[l], lam_init=lam_init, **common)
            ikn = padrows(ikb).reshape(bsz, cf["tn_rows"], idx_hd)
            bias_c, bias_n = _select(iq, iw, ikn, batch=bsz, t=t, tq=cf["tqs"], past=pst,
                                     ikc=caches["ik"] if has_cache else None, layer=l)
            ob = _attn(sq, padrows(skb), padrows(svb), diff=False,
                       cache=(caches["sk"], caches["sv"]) if has_cache else None,
                       bias_c=bias_c, bias_n=bias_n, **common)
            x_mid, h2 = _post(cf["x"], oa, ob, r2(norm_attn)[l], wa_b[l], wb_b[l], wg_b[l], r2(b_gate)[l],
                              wo_b[l], r2(norm_ffn)[l], tm=cf["tm"])
            prev = state_ffn_conv[l] if has_cache else jnp.zeros((bsz, CONV_W - 1, 2 * dff), F32)
            x_new, new_l, new_g = _ffn(x_mid, h2, wup_b[l], conv_w[l], r2(conv_b)[l], prev, wdn_b[l],
                                       batch=bsz, t=t, tm=cf["tmf"], tn=tn_ffn)
            last_tile = lambda a: a.reshape(bsz, t // cf["tmf"], CONV_W - 1, dff)[:, -1]
            cf["x"] = x_new
            conv_state = jnp.concatenate([last_tile(new_l), last_tile(new_g)], axis=-1)
            if stacked:
                cf["rows"].append((conv_state,))
            else:
                cf["rows"].append((dkf.reshape(bsz, t, diff_heads2, diff_hd),
                                   dvf.reshape(bsz, t, diff_heads2 // 2, 2 * diff_hd),
                                   skf.reshape(bsz, t, dsa_heads, dsa_hd),
                                   svf.reshape(bsz, t, dsa_heads, dsa_hd),
                                   ikf.reshape(bsz, t, idx_hd), conv_state))

    outs, stacks = [], []
    for cf in cfgs:
        y = _final_norm(cf["x"], norm_final.reshape(1, -1), tm=cf["tm"])
        outs.append(y.reshape(cf["bsz"], cf["t"], d_model))
        per_layer = [jnp.stack(z, axis=0) for z in zip(*cf["rows"])]
        if "stk" in cf:
            dkf, skf, dvf, svf, ikf = cf["stk"]
            bsz, t = cf["bsz"], cf["t"]
            per_layer = [jnp.transpose(dkf, (0, 1, 4, 2, 3)),
                         dvf.reshape(depth, bsz, t, diff_heads2 // 2, 2 * diff_hd),
                         skf.reshape(depth, bsz, t, dsa_heads, dsa_hd),
                         svf.reshape(depth, bsz, t, dsa_heads, dsa_hd),
                         jnp.transpose(ikf, (0, 1, 3, 2))] + per_layer
        stacks.append(per_layer)
    return (outs[0], outs[1], *stacks[0], *stacks[1])
```

```python
import functools
import math

import numpy as np
import jax
import jax.numpy as jnp
from jax import lax
from jax.experimental import pallas as pl
from jax.experimental.pallas import tpu as pltpu

F32 = jnp.float32
BF16 = jnp.bfloat16
I32 = jnp.int32

CHUNK = 64
CHUNK_SHIFT = 6
IDX_HEADS = 4
TOPK_MAX = 256
CONV_W = 3
ROPE_THETA = 10000.0
EPS = 1e-6

LANES = 128
NEG = -1e30
LOG2E = 1.4426950408889634
INT_MIN = -(2 ** 31)
KEY_NEG_INF = INT_MIN + 0x7FFFFF
KEY_POS_INF = 0x7F800000
VMEM_LIMIT = 56 * 1024 * 1024
SEL_WSTEP = 512


def _cparams(n_axes):
    return pltpu.CompilerParams(dimension_semantics=("arbitrary",) * n_axes,
                                vmem_limit_bytes=VMEM_LIMIT)


def _rms(x, g):
    return x * lax.rsqrt(jnp.mean(x * x, axis=-1, keepdims=True) + EPS) * g


def _dot(a, b):
    return jnp.dot(a, b, preferred_element_type=F32)


def _dot_nt(a, b):
    return lax.dot_general(a, b, (((1,), (1,)), ((), ())), preferred_element_type=F32)


def _pick(n, pref):
    t = min(n, pref)
    while n % t:
        t //= 2
    return t


def _proj_kernel(*refs, dw, sw, idx_hd, stacked):
    (x_ref, g_ref, w64_ref, w128_ref, wv_ref, wikw_ref,
     c64_ref, sa64_ref, sb64_ref, c128_ref, s128_ref) = refs[:11]
    (dq_ref, dkb_ref, dkf_ref, iq_ref, sq_ref, skb_ref, skf_ref,
     dvb_ref, dvf_ref, svb_ref, svf_ref, ikb_ref, ikf_ref, iw_ref) = refs[-14:]
    tm = x_ref.shape[0]
    half = LANES // 2

    def put64(ref, c, r):
        if stacked:
            rt = r.T
            ref[0, 0, c // half] = rt[:half]
            ref[0, 0, c // half + 1] = rt[half:]
        else:
            ref[:, c:c + LANES] = r

    def put128(ref, c, y, width):
        if stacked:
            ref[0, 0, pl.ds(c // LANES, tm, stride=width // LANES), :] = y
        else:
            ref[:, c:c + LANES] = y

    h = _rms(x_ref[...], g_ref[...]).astype(BF16)
    c64, sa64, sb64 = c64_ref[...], sa64_ref[...], sb64_ref[...]
    c128, s128 = c128_ref[...], s128_ref[...]

    def rope64(y):
        return y * c64 + pltpu.roll(y, LANES - 32, 1) * sa64 + pltpu.roll(y, 32, 1) * sb64

    def rope128(y):
        return y * c128 + pltpu.roll(y, 64, 1) * s128

    def slices(w_ref):
        n = w_ref.shape[1]
        for c0 in range(0, n, 2 * LANES):
            y = _dot(h, w_ref[:, c0:c0 + 2 * LANES])
            for c in (c0, c0 + LANES):
                yield c, y[:, c - c0:c - c0 + LANES]

    diff_scale = 64 ** -0.5 * LOG2E
    dsa_scale = LANES ** -0.5 * LOG2E
    idx_scale = 0.125

    for c, y in slices(w64_ref):
        r = rope64(y)
        if c < dw:
            dq_ref[:, c:c + LANES] = (r * diff_scale).astype(BF16)
        elif c < 2 * dw:
            put64(dkf_ref, c - dw, r)
            dkb_ref[:, c - dw:c - dw + LANES] = r.astype(BF16)
        else:
            iq_ref[:, c - 2 * dw:c - 2 * dw + LANES] = r * idx_scale
    for c, y in slices(w128_ref):
        r = rope128(y)
        if c < sw:
            sq_ref[:, c:c + LANES] = (r * dsa_scale).astype(BF16)
        else:
            put128(skf_ref, c - sw, r, sw)
            skb_ref[:, c - sw:c - sw + LANES] = r.astype(BF16)
    for c, y in slices(wv_ref):
        if c < dw:
            put128(dvf_ref, c, y, dw)
            dvb_ref[:, c:c + LANES] = y.astype(BF16)
        else:
            put128(svf_ref, c - dw, y, sw)
            svb_ref[:, c - dw:c - dw + LANES] = y.astype(BF16)
    y = _dot(h, wikw_ref[...])
    r = rope64(y)
    if stacked:
        ikf_ref[0, 0] = r.T[:idx_hd]
    else:
        ikf_ref[...] = r[:, :idx_hd]
    ikb_ref[...] = r[:, :idx_hd].astype(BF16)
    iw_ref[...] = y * (IDX_HEADS ** -0.5)


_STACKED_OUTS = (2, 6, 8, 10, 12)


def _proj(x, g, wts, tabs, *, tm, n_tab, dw, sw, iqw, idx_hd, stack=None):
    n, d = x.shape
    w64, w128, wv, wikw = wts
    row = lambda i: (i, 0)
    const = lambda i: (0, 0)
    tab = lambda i: (i % n_tab, 0)

    def o(width, dtype):
        return jax.ShapeDtypeStruct((n, width), dtype), pl.BlockSpec((tm, width), row)

    outs = [o(dw, BF16), o(dw, BF16), o(dw, F32), o(iqw, F32),
            o(sw, BF16), o(sw, BF16), o(sw, F32),
            o(dw, BF16), o(dw, F32), o(sw, BF16), o(sw, F32),
            o(idx_hd, BF16), o(idx_hd, F32), o(LANES, F32)]
    extra_in, extra_specs, aliases = [], [], {}
    if stack is not None:
        depth, layer, batch, t, prev = stack
        nt = t // tm
        half = LANES // 2

        def wide(width):
            hn = width // LANES
            return (jax.ShapeDtypeStruct((depth, batch, t * hn, LANES), F32),
                    pl.BlockSpec((1, 1, tm * hn, LANES), lambda i: (layer, i // nt, i % nt, 0)))

        outs[2] = (jax.ShapeDtypeStruct((depth, batch, dw // half, half, t), F32),
                   pl.BlockSpec((1, 1, dw // half, half, tm), lambda i: (layer, i // nt, 0, 0, i % nt)))
        outs[6], outs[8], outs[10] = wide(sw), wide(dw), wide(sw)
        outs[12] = (jax.ShapeDtypeStruct((depth, batch, idx_hd, t), F32),
                    pl.BlockSpec((1, 1, idx_hd, tm), lambda i: (layer, i // nt, 0, i % nt)))
        if prev is not None:
            extra_in = list(prev)
            extra_specs = [pl.BlockSpec(memory_space=pl.ANY)] * len(prev)
            aliases = {11 + k: o_idx for k, o_idx in enumerate(_STACKED_OUTS)}
    return pl.pallas_call(
        functools.partial(_proj_kernel, dw=dw, sw=sw, idx_hd=idx_hd, stacked=stack is not None),
        grid=(n // tm,),
        in_specs=[pl.BlockSpec((tm, d), row), pl.BlockSpec((1, d), const)]
        + [pl.BlockSpec(w.shape, const) for w in (w64, w128, wv, wikw)]
        + [pl.BlockSpec((tm, LANES), tab)] * 5 + extra_specs,
        out_specs=[s for _, s in outs],
        out_shape=[s for s, _ in outs],
        input_output_aliases=aliases,
        compiler_params=_cparams(1),
        name="proj",
    )(x, g, w64, w128, wv, wikw, *tabs, *extra_in)


def _attn_kernel(qt_ref, kt_ref, *refs, diff, heads, tq, tkn, tkc, ncb, past, t_valid, lam_init, rc):
    it = iter(refs)
    q_ref = next(it)
    kc_ref = vc_ref = bc_ref = bn_ref = lam_ref = sub_ref = qs_ref = None
    if ncb:
        kc_ref, vc_ref = next(it), next(it)
    kn_ref, vn_ref = next(it), next(it)
    if diff:
        lam_ref, sub_ref = next(it), next(it)
    else:
        if ncb:
            bc_ref = next(it)
        bn_ref = next(it)
    o_ref = next(it)
    if diff:
        qs_ref = next(it)
    m_ref, l_ref, acc_ref, s_ref, p_ref, vx_ref, bias_ref = (next(it) for _ in range(7))

    step_id = pl.program_id(1)
    qi, ki = qt_ref[step_id], kt_ref[step_id]
    rows = 2 * tq if diff else tq
    hs = lambda h: slice(h * LANES, (h + 1) * LANES)

    @pl.when(ki == 0)
    def _():
        m_ref[...] = jnp.full(m_ref.shape, NEG, F32)
        l_ref[...] = jnp.zeros(l_ref.shape, F32)
        acc_ref[...] = jnp.zeros(acc_ref.shape, F32)
        vx_ref[:, :, LANES:] = jnp.ones((heads, vx_ref.shape[1], LANES), BF16)
        if diff:
            lane = lax.broadcasted_iota(I32, (tq, LANES), 1)
            for h in range(heads):
                qh = q_ref[:, hs(h)]
                zero = jnp.zeros_like(qh)
                qs_ref[h, :tq, :] = jnp.where(lane < LANES // 2, qh, zero)
                qs_ref[h, tq:, :] = jnp.where(lane >= LANES // 2, qh, zero)

    brows = bias_ref.shape[0]

    def step(get_k, get_v, tk, biased, k_transposed=False):
        n = 0
        for h in range(heads):
            vx_ref[h, :tk, :LANES] = get_v(h)
            kh = get_k(h)
            for r0 in range(0, rows, rc):
                rs = slice(r0, r0 + rc)
                buf = n % s_ref.shape[0]
                n += 1
                qh = qs_ref[h, rs, :] if diff else q_ref[rs, hs(h)]
                s_ref[buf, :, :tk] = _dot(qh, kh) if k_transposed else _dot_nt(qh, kh)

                def col(c, buf=buf, r0=r0):
                    x = s_ref[buf, :, c * LANES:(c + 1) * LANES]
                    if biased:
                        b0 = r0 % brows
                        x = x + bias_ref[b0:b0 + rc, c * LANES:(c + 1) * LANES]
                    return x

                mx = col(0)
                for c in range(1, tk // LANES):
                    mx = jnp.maximum(mx, col(c))
                m_prev = m_ref[h, rs, :]
                m_new = jnp.maximum(m_prev, jnp.max(mx, axis=1, keepdims=True))
                alpha = jnp.exp2(m_prev - m_new)
                for c in range(tk // LANES):
                    p_ref[buf, :, c * LANES:(c + 1) * LANES] = jnp.exp2(col(c) - m_new).astype(BF16)
                pv = _dot(p_ref[buf, :, :tk], vx_ref[h, :tk, :])
                acc_ref[h, rs, :] = alpha * acc_ref[h, rs, :] + pv[:, :LANES]
                l_ref[h, rs, :] = alpha * l_ref[h, rs, :] + pv[:, LANES:]
                m_ref[h, rs, :] = m_new

    if ncb:
        @pl.when(ki < ncb)
        def _():
            if not diff:
                bias_ref[:, :tkc] = bc_ref[0].astype(F32)

            def cached(ref, i):
                n = ref.shape[2] // tkc
                return ref[0, 0, pl.ds(i, tkc, stride=n), :].astype(BF16)

            if diff:
                pair = lambda h: jnp.concatenate([kc_ref[0, 0, 2 * h], kc_ref[0, 0, 2 * h + 1]],
                                                 axis=0).astype(BF16)
                step(pair, functools.partial(cached, vc_ref), tkc, False, k_transposed=True)
            else:
                step(functools.partial(cached, kc_ref), functools.partial(cached, vc_ref), tkc, True)

    kj = ki - ncb
    get_kn, get_vn = (lambda h: kn_ref[:, hs(h)]), (lambda h: vn_ref[:, hs(h)])

    if diff:
        @pl.when((kj >= 0) & (kj < qi))
        def _():
            step(get_kn, get_vn, tkn, False)

        @pl.when(kj == qi)
        def _():
            r = lax.broadcasted_iota(I32, (tq, tkn), 0)
            c = lax.broadcasted_iota(I32, (tq, tkn), 1)
            ok = ((past + kj * tkn + c) >> CHUNK_SHIFT) <= ((past + qi * tq + r) >> CHUNK_SHIFT)
            if t_valid % tkn:
                ok = ok & (c < t_valid - kj * tkn)
            for rep in range(brows // tq):
                bias_ref[rep * tq:(rep + 1) * tq, :tkn] = jnp.where(ok, 0.0, NEG)
            step(get_kn, get_vn, tkn, True)
    else:
        @pl.when(kj >= 0)
        def _():
            bias_ref[:, :tkn] = bn_ref[0].astype(F32)
            step(get_kn, get_vn, tkn, True)

    @pl.when(kj == qi)
    def _():
        if diff:
            lf = lam_ref[...]
            lam = (jnp.exp(jnp.sum(lf[0:1] * lf[1:2], axis=1, keepdims=True))
                   - jnp.exp(jnp.sum(lf[2:3] * lf[3:4], axis=1, keepdims=True)) + lam_init)
            g = sub_ref[...]
        for h in range(heads):
            o = acc_ref[h] / l_ref[h]
            if diff:
                o = o[:tq] - lam * o[tq:]
                o = _rms(o, g) * (1.0 - lam_init)
            o_ref[:, hs(h)] = o.astype(BF16)


def _attn(q, kn, vn, *, diff, batch, t, tq, tkn, past=0, cache=None, layer=0, tkc=0,
          bias_c=None, bias_n=None, lam=None, subln=None, lam_init=0.0):
    n, width = q.shape
    heads = width // LANES
    nq = t // tq
    ncb = 0 if cache is None else past // tkc
    nkn = kn.shape[0] // batch // tkn
    rows = 2 * tq if diff else tq
    rc = min(rows, 256)
    assert rows % rc == 0 and (rc % tq == 0 or tq % rc == 0)
    steps = [(qi, ki) for qi in range(nq) for ki in range(ncb + qi + 1)]
    qt = jnp.asarray(np.array([s[0] for s in steps], np.int32))
    kt = jnp.asarray(np.array([s[1] for s in steps], np.int32))

    def new_idx(b, s, qt, kt):
        return jnp.maximum(kt[s] - ncb, 0)

    qrow = lambda b, s, qt, kt: (b * nq + qt[s], 0)
    new_blk = lambda b, s, qt, kt: (b * nkn + new_idx(b, s, qt, kt), 0)
    const2 = lambda b, s, qt, kt: (0, 0)
    in_specs = [pl.BlockSpec((tq, width), qrow)]
    args = [q]
    if ncb:
        cblk = lambda b, s, qt, kt: jnp.minimum(kt[s], ncb - 1)
        for c in cache:
            if c.ndim == 5:
                in_specs.append(pl.BlockSpec((1, 1, c.shape[2], c.shape[3], tkc),
                                             lambda b, s, qt, kt: (layer, b, 0, 0, cblk(b, s, qt, kt))))
            else:
                in_specs.append(pl.BlockSpec((1, 1, c.shape[2] // ncb, c.shape[3]),
                                             lambda b, s, qt, kt: (layer, b, cblk(b, s, qt, kt), 0)))
        args += list(cache)
    in_specs += [pl.BlockSpec((tkn, width), new_blk)] * 2
    args += [kn, vn]
    if diff:
        in_specs += [pl.BlockSpec(lam.shape, const2), pl.BlockSpec(subln.shape, const2)]
        args += [lam, subln]
    else:
        if ncb:
            in_specs.append(pl.BlockSpec((1, tq, tkc), lambda b, s, qt, kt: (b, qt[s], jnp.minimum(kt[s], ncb - 1))))
            args.append(bias_c)
        in_specs.append(pl.BlockSpec((1, tq, tkn), lambda b, s, qt, kt: (b, qt[s], new_idx(b, s, qt, kt))))
        args.append(bias_n)
    tkmax = max(tkn, tkc)
    nbuf = min(heads * (rows // rc), max(2, 1024 // rc))
    scratch = ([pltpu.VMEM((heads, rows, LANES), BF16)] if diff else []) + [
        pltpu.VMEM((heads, rows, LANES), F32), pltpu.VMEM((heads, rows, LANES), F32),
        pltpu.VMEM((heads, rows, LANES), F32),
        pltpu.VMEM((nbuf, rc, tkmax), F32), pltpu.VMEM((nbuf, rc, tkmax), BF16),
        pltpu.VMEM((heads, tkmax, 2 * LANES), BF16), pltpu.VMEM((max(tq, rc), tkmax), F32)]
    return pl.pallas_call(
        functools.partial(_attn_kernel, diff=diff, heads=heads, tq=tq, tkn=tkn, tkc=tkc, ncb=ncb,
                          past=past, t_valid=t, lam_init=lam_init, rc=rc),
        grid_spec=pltpu.PrefetchScalarGridSpec(
            num_scalar_prefetch=2, grid=(batch, len(steps)), in_specs=in_specs,
            out_specs=pl.BlockSpec((tq, width), qrow), scratch_shapes=scratch),
        out_shape=jax.ShapeDtypeStruct((n, width), BF16),
        compiler_params=_cparams(2),
        name="attn_diff" if diff else "attn_dsa",
    )(qt, kt, *args)


def _sort_key(score):
    bits = lax.bitcast_convert_type(score, I32)
    return jnp.where(bits < 0, bits ^ 0x7FFFFFFF, bits)


def _index_scores(iq_ref, iw_ref, kmat, idx_hd, k_transposed=False):
    sc = jnp.zeros((iq_ref.shape[0], kmat.shape[1 if k_transposed else 0]), F32)
    for h in range(IDX_HEADS):
        qh = iq_ref[:, h * idx_hd:(h + 1) * idx_hd].astype(BF16)
        w = iw_ref[:, idx_hd + h:idx_hd + h + 1]
        sc = sc + w * jnp.maximum(_dot(qh, kmat) if k_transposed else _dot_nt(qh, kmat), 0.0)
    return sc


def _topk_threshold(count_ge, tq, kf):
    def tbody(i, tu):
        cand = tu | jnp.left_shift(jnp.int32(1), jnp.int32(31) - i)
        return jnp.where(count_ge(cand ^ INT_MIN) >= kf, cand, tu)

    return lax.fori_loop(0, 32, tbody, jnp.zeros((tq, 1), I32)) ^ INT_MIN


def _tie_cutoff(count_eq_before, need, tq, nbits):
    def jbody(i, j):
        cand = j | jnp.left_shift(jnp.int32(1), jnp.int32(nbits - 1) - i)
        return jnp.where(count_eq_before(cand) < need, cand, j)

    return lax.fori_loop(0, nbits, jbody, jnp.zeros((tq, 1), I32))


def _bias_from_keys(key_ref, j_ref, w, *, tq, topk):
    kf = float(topk)
    count = lambda pred: jnp.sum(jnp.where(pred, 1.0, 0.0), axis=1, keepdims=True)
    thr = _topk_threshold(lambda cs: count(key_ref[:, :w] >= cs), tq, kf)
    key = key_ref[:, :w]
    eq = key == thr
    need = kf - count(key > thr)
    excess = (count(eq) > need) & (thr > KEY_NEG_INF)
    j_ref[...] = jnp.full((tq, 1), w, I32)
    idx = lax.broadcasted_iota(I32, (tq, w), 1)

    @pl.when(jnp.max(jnp.where(excess, 1.0, 0.0)) > 0.0)
    def _():
        j_ref[...] = _tie_cutoff(lambda cand: count((key_ref[:, :w] == thr) & (idx < cand)), need, tq,
                                 max(1, (w - 1).bit_length()))

    sel = ((key > thr) | (eq & (idx <= j_ref[...]))) & (key > KEY_NEG_INF) & (key < KEY_POS_INF)
    return jnp.where(sel, 0.0, NEG).astype(BF16)


def _select_kernel(*refs, tq, t, tn, past, topk, idx_hd, wstep):
    it = iter(refs)
    iq_ref, iw_ref, ikn_ref = next(it), next(it), next(it)
    ikc_ref = next(it) if past else None
    bc_ref = next(it) if past else None
    bn_ref, key_ref, j_ref = next(it), next(it), next(it)
    qi = pl.program_id(1)
    qpos = past + qi * tq + lax.broadcasted_iota(I32, (tq, 1), 0)

    def keys(kmat, kpos, valid, k_transposed=False):
        ok = (kpos >> CHUNK_SHIFT) <= (qpos >> CHUNK_SHIFT)
        if valid is not None:
            ok = ok & valid
        sc = _index_scores(iq_ref, iw_ref, kmat, idx_hd, k_transposed)
        return jnp.where(ok, _sort_key(sc), KEY_NEG_INF)

    def run(wn):
        if past:
            key_ref[:, :past] = keys(ikc_ref[0, 0].astype(BF16), lax.broadcasted_iota(I32, (1, past), 1), None,
                                     k_transposed=True)
        c = lax.broadcasted_iota(I32, (1, wn), 1)
        key_ref[:, past:past + wn] = keys(ikn_ref[0, :wn, :], past + c, (c < t) if tn != t else None)
        bias = _bias_from_keys(key_ref, j_ref, past + wn, tq=tq, topk=topk)
        if past:
            bc_ref[0] = bias[:, :past]
        bn_ref[0, :, :wn] = bias[:, past:]
        if wn < tn:
            bn_ref[0, :, wn:] = jnp.full((tq, tn - wn), NEG, BF16)

    ncls = tn // wstep
    if ncls <= 1:
        run(tn)
    else:
        cls = ((qi + 1) * tq + wstep - 1) // wstep - 1
        for j in range(ncls):
            pl.when(cls == j)(functools.partial(run, (j + 1) * wstep))


def _select(iq, iw, ikn, *, batch, t, tq, past=0, ikc=None, layer=0):
    tn, idx_hd = ikn.shape[1], ikn.shape[2]
    nq = t // tq
    topk = min(TOPK_MAX, (past + t) // 4)
    wstep = SEL_WSTEP if (past == 0 and tn == t and tn % SEL_WSTEP == 0 and SEL_WSTEP % tq == 0) else tn
    row = lambda b, qi: (b * nq + qi, 0)
    in_specs = [pl.BlockSpec((tq, iq.shape[1]), row), pl.BlockSpec((tq, LANES), row),
                pl.BlockSpec((1, tn, idx_hd), lambda b, qi: (b, 0, 0))]
    args = [iq, iw, ikn]
    out_specs, out_shape = [], []
    if past:
        in_specs.append(pl.BlockSpec((1, 1, idx_hd, past), lambda b, qi: (layer, b, 0, 0)))
        args.append(ikc)
        out_specs.append(pl.BlockSpec((1, tq, past), lambda b, qi: (b, qi, 0)))
        out_shape.append(jax.ShapeDtypeStruct((batch, t, past), BF16))
    out_specs.append(pl.BlockSpec((1, tq, tn), lambda b, qi: (b, qi, 0)))
    out_shape.append(jax.ShapeDtypeStruct((batch, t, tn), BF16))
    out = pl.pallas_call(
        functools.partial(_select_kernel, tq=tq, t=t, tn=tn, past=past, topk=topk, idx_hd=idx_hd, wstep=wstep),
        grid=(batch, nq),
        in_specs=in_specs, out_specs=out_specs, out_shape=out_shape,
        scratch_shapes=[pltpu.VMEM((tq, past + tn), I32), pltpu.VMEM((tq, 1), I32)],
        compiler_params=_cparams(2),
        name="select",
    )(*args)
    return (out[0], out[1]) if past else (None, out[0])


def _post_kernel(x_ref, oa_ref, ob_ref, ga_ref, wa_ref, wb_ref, wg_ref, bg_ref, wo_ref, gf_ref,
                 xo_ref, h2_ref, mg_ref):
    d = x_ref.shape[1]
    x = x_ref[...]
    h = _rms(x, ga_ref[...]).astype(BF16)
    oa, ob = oa_ref[...], ob_ref[...]
    cw = 2 * LANES
    for c in range(0, d, cw):
        gate_a = jax.nn.sigmoid(_dot(h, wg_ref[:, c:c + cw]) + bg_ref[:, c:c + cw])
        gate_b = jax.nn.sigmoid(_dot(h, wg_ref[:, d + c:d + c + cw]) + bg_ref[:, d + c:d + c + cw])
        merged = gate_a * _dot(oa, wa_ref[:, c:c + cw]) + gate_b * _dot(ob, wb_ref[:, c:c + cw])
        mg_ref[:, c:c + cw] = merged.astype(BF16)
    mg = mg_ref[...]
    for c in range(0, d, cw):
        xo_ref[:, c:c + cw] = x[:, c:c + cw] + _dot(mg, wo_ref[:, c:c + cw])
    h2_ref[...] = _rms(xo_ref[...], gf_ref[...]).astype(BF16)


def _post(x, oa, ob, ga, wa, wb, wg, bg, wo, gf, *, tm):
    n, d = x.shape
    row = lambda i: (i, 0)
    const = lambda i: (0, 0)
    full = lambda a: pl.BlockSpec(a.shape, const)
    return pl.pallas_call(
        _post_kernel,
        grid=(n // tm,),
        in_specs=[pl.BlockSpec((tm, d), row)] * 3 + [full(a) for a in (ga, wa, wb, wg, bg, wo, gf)],
        out_specs=[pl.BlockSpec((tm, d), row)] * 2,
        out_shape=[jax.ShapeDtypeStruct((n, d), F32), jax.ShapeDtypeStruct((n, d), BF16)],
        scratch_shapes=[pltpu.VMEM((tm, d), BF16)],
        compiler_params=_cparams(1),
        name="post",
    )(x, oa, ob, ga, wa, wb, wg, bg, wo, gf)


def _ffn_kernel(x_ref, h_ref, wl_ref, wgt_ref, cwl_ref, cwg_ref, cbl_ref, cbg_ref, pl_ref, pg_ref,
                wd_ref, xo_ref, nl_ref, ng_ref, acc_ref, cl_ref, cg_ref, *, tm, seqs):
    i, j = pl.program_id(1), pl.program_id(2)
    nj = pl.num_programs(2)
    h = h_ref[...]
    rid = lax.broadcasted_iota(I32, (tm, 1), 0)

    def conv(w_ref, cw_ref, cb_ref, prev_ref, carry_ref, new_ref):
        u = _dot(h, w_ref[...])
        u1, u2 = pltpu.roll(u, 1, 0), pltpu.roll(u, 2, 0)
        cw = cw_ref[...]
        if seqs == 1:
            hal = jnp.where(i == 0, prev_ref[0], carry_ref[j, 6:8, :])
            u1 = jnp.where(rid == 0, hal[1:2], u1)
            u2 = jnp.where(rid == 0, hal[0:1], jnp.where(rid == 1, hal[1:2], u2))
            carry_ref[j] = u[tm - 8:, :]
            new_ref[0] = u[tm - 2:, :]
        else:
            ts = tm // seqs
            for s in range(seqs):
                hal = prev_ref[s]
                u1 = jnp.where(rid == s * ts, hal[1:2], u1)
                u2 = jnp.where(rid == s * ts, hal[0:1], jnp.where(rid == s * ts + 1, hal[1:2], u2))
                new_ref[s] = u[(s + 1) * ts - 2:(s + 1) * ts, :]
        return cb_ref[...] + u2 * cw[0:1] + u1 * cw[1:2] + u * cw[2:3]

    c_lin = conv(wl_ref, cwl_ref, cbl_ref, pl_ref, cl_ref, nl_ref)
    c_gate = conv(wgt_ref, cwg_ref, cbg_ref, pg_ref, cg_ref, ng_ref)
    act = (jax.nn.silu(c_gate) * c_lin).astype(BF16)
    part = _dot(act, wd_ref[...])

    @pl.when(j == 0)
    def _():
        acc_ref[...] = part

    @pl.when(j > 0)
    def _():
        acc_ref[...] += part

    @pl.when(j == nj - 1)
    def _():
        xo_ref[...] = x_ref[...] + acc_ref[...]


def _ffn(x, h2, w_up, conv_w, conv_b, conv_prev, w_down, *, batch, t, tm, tn, seqs=1):
    n, d = x.shape
    dff = w_down.shape[0]
    nt, nj = t // tm, dff // tn
    assert seqs == 1 or (nt == 1 and batch % seqs == 0)
    nb, tm = batch // seqs, tm * seqs
    row = lambda b, i, j: (b * nt + i, 0)
    lin = lambda b, i, j: (0, j)
    gate = lambda b, i, j: (0, nj + j)
    tile_rows = lambda b, i, j: (b * nt + i, 0, j)
    return pl.pallas_call(
        functools.partial(_ffn_kernel, tm=tm, seqs=seqs),
        grid=(nb, nt, nj),
        in_specs=[pl.BlockSpec((tm, d), row), pl.BlockSpec((tm, d), row),
                  pl.BlockSpec((d, tn), lin), pl.BlockSpec((d, tn), gate),
                  pl.BlockSpec((CONV_W, tn), lin), pl.BlockSpec((CONV_W, tn), gate),
                  pl.BlockSpec((1, tn), lin), pl.BlockSpec((1, tn), gate),
                  pl.BlockSpec((seqs, CONV_W - 1, tn), lambda b, i, j: (b, 0, j)),
                  pl.BlockSpec((seqs, CONV_W - 1, tn), lambda b, i, j: (b, 0, nj + j)),
                  pl.BlockSpec((tn, d), lambda b, i, j: (j, 0))],
        out_specs=[pl.BlockSpec((tm, d), row),
                   pl.BlockSpec((seqs, CONV_W - 1, tn), tile_rows),
                   pl.BlockSpec((seqs, CONV_W - 1, tn), tile_rows)],
        out_shape=[jax.ShapeDtypeStruct((n, d), F32),
                   jax.ShapeDtypeStruct((batch * nt, CONV_W - 1, dff), F32),
                   jax.ShapeDtypeStruct((batch * nt, CONV_W - 1, dff), F32)],
        scratch_shapes=[pltpu.VMEM((tm, d), F32), pltpu.VMEM((nj, 8, tn), F32),
                        pltpu.VMEM((nj, 8, tn), F32)],
        compiler_params=_cparams(3),
        name="ffn",
    )(x, h2, w_up, w_up, conv_w, conv_w, conv_b, conv_b, conv_prev, conv_prev, w_down)


def _final_kernel(x_ref, g_ref, o_ref):
    o_ref[...] = _rms(x_ref[...], g_ref[...])


def _final_norm(x, g, *, tm):
    n, d = x.shape
    return pl.pallas_call(
        _final_kernel, grid=(n // tm,),
        in_specs=[pl.BlockSpec((tm, d), lambda i: (i, 0)), pl.BlockSpec((1, d), lambda i: (0, 0))],
        out_specs=pl.BlockSpec((tm, d), lambda i: (i, 0)),
        out_shape=jax.ShapeDtypeStruct((n, d), F32),
        compiler_params=_cparams(1), name="final_norm",
    )(x, g)


def _rope_tables(pos, tm):
    def ang(hd):
        half = hd // 2
        inv = ROPE_THETA ** (-(jnp.arange(half, dtype=F32) / half))
        return pos.astype(F32)[:, None] * inv[None, :]

    a64, a128 = ang(64), ang(128)
    cos64 = jnp.tile(jnp.cos(a64), (1, 4))
    sin64 = jnp.tile(jnp.sin(a64), (1, 4))
    lane = jnp.arange(LANES)[None, :]
    low64 = (lane % 64) < 32
    sa64 = jnp.where(low64, -sin64, 0.0)
    sb64 = jnp.where(low64, 0.0, sin64)
    cos128 = jnp.tile(jnp.cos(a128), (1, 2))
    sin128 = jnp.tile(jnp.sin(a128), (1, 2))
    s128 = jnp.where(lane < 64, -sin128, sin128)
    tabs = (cos64, sa64, sb64, cos128, s128)
    reps = -(-tm // pos.shape[0])
    return tuple(jnp.tile(tb, (reps, 1)) for tb in tabs)


def _ffn_chunk(dff):
    best = LANES
    for k in range(1, dff // LANES + 1):
        if dff % (k * LANES) == 0 and k * LANES <= 1536:
            best = k * LANES
    return best


def kernel(x_prompt, x_sample, cache_diff_k, cache_diff_v, cache_dsa_k, cache_dsa_v, cache_idx_k,
           state_ffn_conv, norm_attn, w_in, diff_lambda, diff_subln, w_branch_a, w_branch_b, w_gate,
           b_gate, w_out, norm_ffn, w_up, conv_w, conv_b, w_down, norm_final):
    depth, dec_batch, past = cache_diff_k.shape[:3]
    batch, seq, d_model = x_prompt.shape
    dec_seq = x_sample.shape[1]
    diff_heads2, diff_hd = cache_diff_k.shape[3:]
    dsa_heads, dsa_hd = cache_dsa_k.shape[3:]
    idx_hd = cache_idx_k.shape[3]
    dw, sw, iqw = diff_heads2 * diff_hd, dsa_heads * dsa_hd, IDX_HEADS * idx_hd
    dff = w_down.shape[1]
    assert diff_hd == 64 and idx_hd == 64 and dsa_hd == LANES and 2 * diff_hd == LANES
    assert w_in.shape[2] == 3 * dw + 3 * sw + iqw + idx_hd + IDX_HEADS
    assert seq % CHUNK == 0 and dec_seq % CHUNK == 0 and past % CHUNK == 0

    o = [0]
    for size in (dw, dw, dw, sw, sw, sw, iqw, idx_hd, IDX_HEADS):
        o.append(o[-1] + size)
    wb = w_in.astype(BF16)
    w64 = jnp.concatenate([wb[:, :, o[0]:o[2]], wb[:, :, o[6]:o[7]]], axis=2)
    w128 = wb[:, :, o[3]:o[5]]
    wv = jnp.concatenate([wb[:, :, o[2]:o[3]], wb[:, :, o[5]:o[6]]], axis=2)
    wikw = jnp.pad(wb[:, :, o[7]:o[9]], ((0, 0), (0, 0), (0, LANES - idx_hd - IDX_HEADS)))
    wa_b, wb_b, wg_b, wo_b = (w.astype(BF16) for w in (w_branch_a, w_branch_b, w_gate, w_out))
    wup_b, wdn_b = w_up.astype(BF16), w_down.astype(BF16)
    r2 = lambda a: a.reshape(a.shape[0], 1, a.shape[1])

    cfgs = []
    for name, x, bsz, t, pst in (("p", x_prompt, batch, seq, 0), ("s", x_sample, dec_batch, dec_seq, past)):
        n = bsz * t
        tm = _pick(n, 256)
        tq = _pick(t, 512)
        tkn = max(tq, LANES)
        pos = pst + jnp.arange(t, dtype=jnp.int32)
        cfgs.append(dict(name=name, x=x.reshape(n, d_model), bsz=bsz, t=t, past=pst, n=n, tm=tm, tq=tq,
                         tkn=tkn, tn_rows=max(t, tkn), tabs=_rope_tables(pos, tm), n_tab=max(t // tm, 1),
                         tqs=_pick(t, 512), tmf=_pick(t, 512), rows=[]))

    caches = None
    if past:
        flat = lambda a: a.reshape(depth, dec_batch, past * a.shape[3], a.shape[4])
        caches = dict(dk=jnp.transpose(cache_diff_k, (0, 1, 3, 4, 2)), dv=flat(cache_diff_v),
                      sk=flat(cache_dsa_k), sv=flat(cache_dsa_v),
                      ik=jnp.transpose(cache_idx_k, (0, 1, 3, 2)))
    tkc = _pick(past, 1024) if past else 0
    tn_ffn = _ffn_chunk(dff)

    for l in range(depth):
        lam_init = 0.8 - 0.6 * math.exp(-0.3 * l)
        for cf in cfgs:
            bsz, t, pst, n = cf["bsz"], cf["t"], cf["past"], cf["n"]
            stacked = t % cf["tm"] == 0 and cf["tm"] % LANES == 0
            (dq, dkb, dkf, iq, sq, skb, skf, dvb, dvf, svb, svf, ikb, ikf, iw) = _proj(
                cf["x"], r2(norm_attn)[l], (w64[l], w128[l], wv[l], wikw[l]), cf["tabs"],
                tm=cf["tm"], n_tab=cf["n_tab"], dw=dw, sw=sw, iqw=iqw, idx_hd=idx_hd,
                stack=(depth, l, bsz, t, cf.get("stk")) if stacked else None)
            if stacked:
                cf["stk"] = (dkf, skf, dvf, svf, ikf)

            def padrows(a):
                if cf["tn_rows"] == t:
                    return a
                a = a.reshape(bsz, t, a.shape[1])
                a = jnp.pad(a, ((0, 0), (0, cf["tn_rows"] - t), (0, 0)))
                return a.reshape(bsz * cf["tn_rows"], -1)

            has_cache = bool(pst)
            common = dict(batch=bsz, t=t, tq=cf["tq"], tkn=cf["tkn"], past=pst, layer=l, tkc=tkc)
            oa = _attn(dq, padrows(dkb), padrows(dvb), diff=True,
                       cache=(caches["dk"], caches["dv"]) if has_cache else None,
                       lam=diff_lambda[l], subln=r2(diff_subln)[l], lam_init=lam_init, **common)
            ikn = padrows(ikb).reshape(bsz, cf["tn_rows"], idx_hd)
            bias_c, bias_n = _select(iq, iw, ikn, batch=bsz, t=t, tq=cf["tqs"], past=pst,
                                     ikc=caches["ik"] if has_cache else None, layer=l)
            ob = _attn(sq, padrows(skb), padrows(svb), diff=False,
                       cache=(caches["sk"], caches["sv"]) if has_cache else None,
                       bias_c=bias_c, bias_n=bias_n, **common)
            x_mid, h2 = _post(cf["x"], oa, ob, r2(norm_attn)[l], wa_b[l], wb_b[l], wg_b[l], r2(b_gate)[l],
                              wo_b[l], r2(norm_ffn)[l], tm=cf["tm"])
            prev = state_ffn_conv[l] if has_cache else jnp.zeros((bsz, CONV_W - 1, 2 * dff), F32)
            seqs = 1
            if cf["tmf"] == t:
                seqs = max(g for g in (1, 2, 4, 8) if bsz % g == 0 and g * t <= 256)
            x_new, new_l, new_g = _ffn(x_mid, h2, wup_b[l], conv_w[l], r2(conv_b)[l], prev, wdn_b[l],
                                       batch=bsz, t=t, tm=cf["tmf"], tn=tn_ffn, seqs=seqs)
            last_tile = lambda a: a.reshape(bsz, t // cf["tmf"], CONV_W - 1, dff)[:, -1]
            cf["x"] = x_new
            conv_state = jnp.concatenate([last_tile(new_l), last_tile(new_g)], axis=-1)
            if stacked:
                cf["rows"].append((conv_state,))
            else:
                cf["rows"].append((dkf.reshape(bsz, t, diff_heads2, diff_hd),
                                   dvf.reshape(bsz, t, diff_heads2 // 2, 2 * diff_hd),
                                   skf.reshape(bsz, t, dsa_heads, dsa_hd),
                                   svf.reshape(bsz, t, dsa_heads, dsa_hd),
                                   ikf.reshape(bsz, t, idx_hd), conv_state))

    outs, stacks = [], []
    for cf in cfgs:
        y = _final_norm(cf["x"], norm_final.reshape(1, -1), tm=cf["tm"])
        outs.append(y.reshape(cf["bsz"], cf["t"], d_model))
        per_layer = [jnp.stack(z, axis=0) for z in zip(*cf["rows"])]
        if "stk" in cf:
            dkf, skf, dvf, svf, ikf = cf["stk"]
            bsz, t = cf["bsz"], cf["t"]
            per_layer = [jnp.transpose(dkf, (0, 1, 4, 2, 3)),
                         dvf.reshape(depth, bsz, t, diff_heads2 // 2, 2 * diff_hd),
                         skf.reshape(depth, bsz, t, dsa_heads, dsa_hd),
                         svf.reshape(depth, bsz, t, dsa_heads, dsa_hd),
                         jnp.transpose(ikf, (0, 1, 3, 2))] + per_layer
        stacks.append(per_layer)
    return (outs[0], outs[1], *stacks[0], *stacks[1])
```

```python
import functools
import math

import numpy as np
import jax
import jax.numpy as jnp
from jax import lax
from jax.experimental import pallas as pl
from jax.experimental.pallas import tpu as pltpu

F32 = jnp.float32
BF16 = jnp.bfloat16
I32 = jnp.int32

CHUNK = 64
CHUNK_SHIFT = 6
IDX_HEADS = 4
TOPK_MAX = 256
CONV_W = 3
ROPE_THETA = 10000.0
EPS = 1e-6

LANES = 128
NEG = -1e30
LOG2E = 1.4426950408889634
INT_MIN = -(2 ** 31)
KEY_NEG_INF = INT_MIN + 0x7FFFFF
KEY_POS_INF = 0x7F800000
VMEM_LIMIT = 56 * 1024 * 1024
SEL_WSTEP = 512


def _cparams(n_axes):
    return pltpu.CompilerParams(dimension_semantics=("arbitrary",) * n_axes,
                                vmem_limit_bytes=VMEM_LIMIT)


def _rms(x, g):
    return x * lax.rsqrt(jnp.mean(x * x, axis=-1, keepdims=True) + EPS) * g


def _dot(a, b):
    return jnp.dot(a, b, preferred_element_type=F32)


def _dot_nt(a, b):
    return lax.dot_general(a, b, (((1,), (1,)), ((), ())), preferred_element_type=F32)


def _pick(n, pref):
    t = min(n, pref)
    while n % t:
        t //= 2
    return t


def _proj_kernel(*refs, dw, sw, idx_hd, stacked):
    (x_ref, g_ref, w64_ref, w128_ref, wv_ref, wikw_ref,
     c64_ref, sa64_ref, sb64_ref, c128_ref, s128_ref) = refs[:11]
    (dq_ref, dkb_ref, dkf_ref, iq_ref, sq_ref, skb_ref, skf_ref,
     dvb_ref, dvf_ref, svb_ref, svf_ref, ikb_ref, ikf_ref, iw_ref) = refs[-14:]
    tm = x_ref.shape[0]
    half = LANES // 2

    def put64(ref, c, r):
        if stacked:
            rt = r.T
            ref[0, 0, c // half] = rt[:half]
            ref[0, 0, c // half + 1] = rt[half:]
        else:
            ref[:, c:c + LANES] = r

    def put128(ref, c, y, width):
        if stacked:
            ref[0, 0, pl.ds(c // LANES, tm, stride=width // LANES), :] = y
        else:
            ref[:, c:c + LANES] = y

    h = _rms(x_ref[...], g_ref[...]).astype(BF16)
    c64, sa64, sb64 = c64_ref[...], sa64_ref[...], sb64_ref[...]
    c128, s128 = c128_ref[...], s128_ref[...]

    def rope64(y):
        return y * c64 + pltpu.roll(y, LANES - 32, 1) * sa64 + pltpu.roll(y, 32, 1) * sb64

    def rope128(y):
        return y * c128 + pltpu.roll(y, 64, 1) * s128

    def slices(w_ref):
        n = w_ref.shape[1]
        for c0 in range(0, n, 2 * LANES):
            y = _dot(h, w_ref[:, c0:c0 + 2 * LANES])
            for c in (c0, c0 + LANES):
                yield c, y[:, c - c0:c - c0 + LANES]

    diff_scale = 64 ** -0.5 * LOG2E
    dsa_scale = LANES ** -0.5 * LOG2E
    idx_scale = 0.125

    for c, y in slices(w64_ref):
        r = rope64(y)
        if c < dw:
            dq_ref[:, c:c + LANES] = (r * diff_scale).astype(BF16)
        elif c < 2 * dw:
            put64(dkf_ref, c - dw, r)
            dkb_ref[:, c - dw:c - dw + LANES] = r.astype(BF16)
        else:
            iq_ref[:, c - 2 * dw:c - 2 * dw + LANES] = r * idx_scale
    for c, y in slices(w128_ref):
        r = rope128(y)
        if c < sw:
            sq_ref[:, c:c + LANES] = (r * dsa_scale).astype(BF16)
        else:
            put128(skf_ref, c - sw, r, sw)
            skb_ref[:, c - sw:c - sw + LANES] = r.astype(BF16)
    for c, y in slices(wv_ref):
        if c < dw:
            put128(dvf_ref, c, y, dw)
            dvb_ref[:, c:c + LANES] = y.astype(BF16)
        else:
            put128(svf_ref, c - dw, y, sw)
            svb_ref[:, c - dw:c - dw + LANES] = y.astype(BF16)
    y = _dot(h, wikw_ref[...])
    r = rope64(y)
    if stacked:
        ikf_ref[0, 0] = r.T[:idx_hd]
    else:
        ikf_ref[...] = r[:, :idx_hd]
    ikb_ref[...] = r[:, :idx_hd].astype(BF16)
    iw_ref[...] = y * (IDX_HEADS ** -0.5)


_STACKED_OUTS = (2, 6, 8, 10, 12)


def _proj(x, g, wts, tabs, *, tm, n_tab, dw, sw, iqw, idx_hd, stack=None):
    n, d = x.shape
    w64, w128, wv, wikw = wts
    row = lambda i: (i, 0)
    const = lambda i: (0, 0)
    tab = lambda i: (i % n_tab, 0)

    def o(width, dtype):
        return jax.ShapeDtypeStruct((n, width), dtype), pl.BlockSpec((tm, width), row)

    outs = [o(dw, BF16), o(dw, BF16), o(dw, F32), o(iqw, F32),
            o(sw, BF16), o(sw, BF16), o(sw, F32),
            o(dw, BF16), o(dw, F32), o(sw, BF16), o(sw, F32),
            o(idx_hd, BF16), o(idx_hd, F32), o(LANES, F32)]
    extra_in, extra_specs, aliases = [], [], {}
    if stack is not None:
        depth, layer, batch, t, prev = stack
        nt = t // tm
        half = LANES // 2

        def wide(width):
            hn = width // LANES
            return (jax.ShapeDtypeStruct((depth, batch, t * hn, LANES), F32),
                    pl.BlockSpec((1, 1, tm * hn, LANES), lambda i: (layer, i // nt, i % nt, 0)))

        outs[2] = (jax.ShapeDtypeStruct((depth, batch, dw // half, half, t), F32),
                   pl.BlockSpec((1, 1, dw // half, half, tm), lambda i: (layer, i // nt, 0, 0, i % nt)))
        outs[6], outs[8], outs[10] = wide(sw), wide(dw), wide(sw)
        outs[12] = (jax.ShapeDtypeStruct((depth, batch, idx_hd, t), F32),
                    pl.BlockSpec((1, 1, idx_hd, tm), lambda i: (layer, i // nt, 0, i % nt)))
        if prev is not None:
            extra_in = list(prev)
            extra_specs = [pl.BlockSpec(memory_space=pl.ANY)] * len(prev)
            aliases = {11 + k: o_idx for k, o_idx in enumerate(_STACKED_OUTS)}
    return pl.pallas_call(
        functools.partial(_proj_kernel, dw=dw, sw=sw, idx_hd=idx_hd, stacked=stack is not None),
        grid=(n // tm,),
        in_specs=[pl.BlockSpec((tm, d), row), pl.BlockSpec((1, d), const)]
        + [pl.BlockSpec(w.shape, const) for w in (w64, w128, wv, wikw)]
        + [pl.BlockSpec((tm, LANES), tab)] * 5 + extra_specs,
        out_specs=[s for _, s in outs],
        out_shape=[s for s, _ in outs],
        input_output_aliases=aliases,
        compiler_params=_cparams(1),
        name="proj",
    )(x, g, w64, w128, wv, wikw, *tabs, *extra_in)


def _attn_kernel(qt_ref, kt_ref, *refs, diff, heads, tq, tkn, tkc, ncb, past, t_valid, lam_init, rc):
    it = iter(refs)
    q_ref = next(it)
    kc_ref = vc_ref = bc_ref = bn_ref = lam_ref = sub_ref = qs_ref = None
    if ncb:
        kc_ref, vc_ref = next(it), next(it)
    kn_ref, vn_ref = next(it), next(it)
    if diff:
        lam_ref, sub_ref = next(it), next(it)
    else:
        if ncb:
            bc_ref = next(it)
        bn_ref = next(it)
    o_ref = next(it)
    if diff:
        qs_ref = next(it)
    m_ref, l_ref, acc_ref, s_ref, p_ref, vx_ref, bias_ref = (next(it) for _ in range(7))

    step_id = pl.program_id(1)
    qi, ki = qt_ref[step_id], kt_ref[step_id]
    rows = 2 * tq if diff else tq
    hs = lambda h: slice(h * LANES, (h + 1) * LANES)

    @pl.when(ki == 0)
    def _():
        m_ref[...] = jnp.full(m_ref.shape, NEG, F32)
        l_ref[...] = jnp.zeros(l_ref.shape, F32)
        acc_ref[...] = jnp.zeros(acc_ref.shape, F32)
        vx_ref[:, :, LANES:] = jnp.ones((heads, vx_ref.shape[1], LANES), BF16)
        if diff:
            lane = lax.broadcasted_iota(I32, (tq, LANES), 1)
            for h in range(heads):
                qh = q_ref[:, hs(h)]
                zero = jnp.zeros_like(qh)
                qs_ref[h, :tq, :] = jnp.where(lane < LANES // 2, qh, zero)
                qs_ref[h, tq:, :] = jnp.where(lane >= LANES // 2, qh, zero)

    brows = bias_ref.shape[0]

    def step(get_k, get_v, tk, biased, k_transposed=False):
        n = 0
        for h in range(heads):
            vx_ref[h, :tk, :LANES] = get_v(h)
            kh = get_k(h)
            for r0 in range(0, rows, rc):
                rs = slice(r0, r0 + rc)
                buf = n % s_ref.shape[0]
                n += 1
                qh = qs_ref[h, rs, :] if diff else q_ref[rs, hs(h)]
                s_ref[buf, :, :tk] = _dot(qh, kh) if k_transposed else _dot_nt(qh, kh)

                def col(c, buf=buf, r0=r0):
                    x = s_ref[buf, :, c * LANES:(c + 1) * LANES]
                    if biased:
                        b0 = r0 % brows
                        x = x + bias_ref[b0:b0 + rc, c * LANES:(c + 1) * LANES]
                    return x

                mx = col(0)
                for c in range(1, tk // LANES):
                    mx = jnp.maximum(mx, col(c))
                m_prev = m_ref[h, rs, :]
                m_new = jnp.maximum(m_prev, jnp.max(mx, axis=1, keepdims=True))
                alpha = jnp.exp2(m_prev - m_new)
                for c in range(tk // LANES):
                    p_ref[buf, :, c * LANES:(c + 1) * LANES] = jnp.exp2(col(c) - m_new).astype(BF16)
                pv = _dot(p_ref[buf, :, :tk], vx_ref[h, :tk, :])
                acc_ref[h, rs, :] = alpha * acc_ref[h, rs, :] + pv[:, :LANES]
                l_ref[h, rs, :] = alpha * l_ref[h, rs, :] + pv[:, LANES:]
                m_ref[h, rs, :] = m_new

    if ncb:
        @pl.when(ki < ncb)
        def _():
            if not diff:
                bias_ref[:, :tkc] = bc_ref[0].astype(F32)

            def cached(ref, i):
                n = ref.shape[2] // tkc
                return ref[0, 0, pl.ds(i, tkc, stride=n), :].astype(BF16)

            if diff:
                pair = lambda h: jnp.concatenate([kc_ref[0, 0, 2 * h], kc_ref[0, 0, 2 * h + 1]],
                                                 axis=0).astype(BF16)
                step(pair, functools.partial(cached, vc_ref), tkc, False, k_transposed=True)
            else:
                step(functools.partial(cached, kc_ref), functools.partial(cached, vc_ref), tkc, True)

    kj = ki - ncb
    get_kn, get_vn = (lambda h: kn_ref[:, hs(h)]), (lambda h: vn_ref[:, hs(h)])

    if diff:
        @pl.when((kj >= 0) & (kj < qi))
        def _():
            step(get_kn, get_vn, tkn, False)

        @pl.when(kj == qi)
        def _():
            r = lax.broadcasted_iota(I32, (tq, tkn), 0)
            c = lax.broadcasted_iota(I32, (tq, tkn), 1)
            ok = ((past + kj * tkn + c) >> CHUNK_SHIFT) <= ((past + qi * tq + r) >> CHUNK_SHIFT)
            if t_valid % tkn:
                ok = ok & (c < t_valid - kj * tkn)
            for rep in range(brows // tq):
                bias_ref[rep * tq:(rep + 1) * tq, :tkn] = jnp.where(ok, 0.0, NEG)
            step(get_kn, get_vn, tkn, True)
    else:
        @pl.when(kj >= 0)
        def _():
            bias_ref[:, :tkn] = bn_ref[0].astype(F32)
            step(get_kn, get_vn, tkn, True)

    @pl.when(kj == qi)
    def _():
        if diff:
            lf = lam_ref[...]
            lam = (jnp.exp(jnp.sum(lf[0:1] * lf[1:2], axis=1, keepdims=True))
                   - jnp.exp(jnp.sum(lf[2:3] * lf[3:4], axis=1, keepdims=True)) + lam_init)
            g = sub_ref[...]
        for h in range(heads):
            o = acc_ref[h] / l_ref[h]
            if diff:
                o = o[:tq] - lam * o[tq:]
                o = _rms(o, g) * (1.0 - lam_init)
            o_ref[:, hs(h)] = o.astype(BF16)


def _attn(q, kn, vn, *, diff, batch, t, tq, tkn, past=0, cache=None, layer=0, tkc=0,
          bias_c=None, bias_n=None, lam=None, subln=None, lam_init=0.0):
    n, width = q.shape
    heads = width // LANES
    nq = t // tq
    ncb = 0 if cache is None else past // tkc
    nkn = kn.shape[0] // batch // tkn
    rows = 2 * tq if diff else tq
    rc = min(rows, 256 if diff else 512)
    assert rows % rc == 0 and (rc % tq == 0 or tq % rc == 0)
    steps = [(qi, ki) for qi in range(nq) for ki in range(ncb + qi + 1)]
    qt = jnp.asarray(np.array([s[0] for s in steps], np.int32))
    kt = jnp.asarray(np.array([s[1] for s in steps], np.int32))

    def new_idx(b, s, qt, kt):
        return jnp.maximum(kt[s] - ncb, 0)

    qrow = lambda b, s, qt, kt: (b * nq + qt[s], 0)
    new_blk = lambda b, s, qt, kt: (b * nkn + new_idx(b, s, qt, kt), 0)
    const2 = lambda b, s, qt, kt: (0, 0)
    in_specs = [pl.BlockSpec((tq, width), qrow)]
    args = [q]
    if ncb:
        cblk = lambda b, s, qt, kt: jnp.minimum(kt[s], ncb - 1)
        for c in cache:
            if c.ndim == 5:
                in_specs.append(pl.BlockSpec((1, 1, c.shape[2], c.shape[3], tkc),
                                             lambda b, s, qt, kt: (layer, b, 0, 0, cblk(b, s, qt, kt))))
            else:
                in_specs.append(pl.BlockSpec((1, 1, c.shape[2] // ncb, c.shape[3]),
                                             lambda b, s, qt, kt: (layer, b, cblk(b, s, qt, kt), 0)))
        args += list(cache)
    in_specs += [pl.BlockSpec((tkn, width), new_blk)] * 2
    args += [kn, vn]
    if diff:
        in_specs += [pl.BlockSpec(lam.shape, const2), pl.BlockSpec(subln.shape, const2)]
        args += [lam, subln]
    else:
        if ncb:
            in_specs.append(pl.BlockSpec((1, tq, tkc), lambda b, s, qt, kt: (b, qt[s], jnp.minimum(kt[s], ncb - 1))))
            args.append(bias_c)
        in_specs.append(pl.BlockSpec((1, tq, tkn), lambda b, s, qt, kt: (b, qt[s], new_idx(b, s, qt, kt))))
        args.append(bias_n)
    tkmax = max(tkn, tkc)
    nbuf = min(heads * (rows // rc), max(2, 1024 // rc))
    scratch = ([pltpu.VMEM((heads, rows, LANES), BF16)] if diff else []) + [
        pltpu.VMEM((heads, rows, LANES), F32), pltpu.VMEM((heads, rows, LANES), F32),
        pltpu.VMEM((heads, rows, LANES), F32),
        pltpu.VMEM((nbuf, rc, tkmax), F32), pltpu.VMEM((nbuf, rc, tkmax), BF16),
        pltpu.VMEM((heads, tkmax, 2 * LANES), BF16), pltpu.VMEM((max(tq, rc), tkmax), F32)]
    return pl.pallas_call(
        functools.partial(_attn_kernel, diff=diff, heads=heads, tq=tq, tkn=tkn, tkc=tkc, ncb=ncb,
                          past=past, t_valid=t, lam_init=lam_init, rc=rc),
        grid_spec=pltpu.PrefetchScalarGridSpec(
            num_scalar_prefetch=2, grid=(batch, len(steps)), in_specs=in_specs,
            out_specs=pl.BlockSpec((tq, width), qrow), scratch_shapes=scratch),
        out_shape=jax.ShapeDtypeStruct((n, width), BF16),
        compiler_params=_cparams(2),
        name="attn_diff" if diff else "attn_dsa",
    )(qt, kt, *args)


def _sort_key(score):
    bits = lax.bitcast_convert_type(score, I32)
    return jnp.where(bits < 0, bits ^ 0x7FFFFFFF, bits)


def _index_scores(iq_ref, iw_ref, rs, kmat, idx_hd, k_transposed=False):
    sc = jnp.zeros((rs.stop - rs.start, kmat.shape[1 if k_transposed else 0]), F32)
    for h in range(IDX_HEADS):
        qh = iq_ref[rs, h * idx_hd:(h + 1) * idx_hd].astype(BF16)
        w = iw_ref[rs, idx_hd + h:idx_hd + h + 1]
        sc = sc + w * jnp.maximum(_dot(qh, kmat) if k_transposed else _dot_nt(qh, kmat), 0.0)
    return sc


def _topk_threshold(count_ge, tq, kf):
    def tbody(i, tu):
        cand = tu | jnp.left_shift(jnp.int32(1), jnp.int32(31) - i)
        return jnp.where(count_ge(cand ^ INT_MIN) >= kf, cand, tu)

    return lax.fori_loop(0, 32, tbody, jnp.zeros((tq, 1), I32)) ^ INT_MIN


def _tie_cutoff(count_eq_before, need, tq, nbits):
    def jbody(i, j):
        cand = j | jnp.left_shift(jnp.int32(1), jnp.int32(nbits - 1) - i)
        return jnp.where(count_eq_before(cand) < need, cand, j)

    return lax.fori_loop(0, nbits, jbody, jnp.zeros((tq, 1), I32))


def _bias_from_keys(key_ref, j_ref, w, *, tq, topk):
    kf = float(topk)
    count = lambda pred: jnp.sum(jnp.where(pred, 1.0, 0.0), axis=1, keepdims=True)
    thr = _topk_threshold(lambda cs: count(key_ref[:, :w] >= cs), tq, kf)
    key = key_ref[:, :w]
    eq = key == thr
    need = kf - count(key > thr)
    excess = (count(eq) > need) & (thr > KEY_NEG_INF)
    j_ref[...] = jnp.full((tq, 1), w, I32)
    idx = lax.broadcasted_iota(I32, (tq, w), 1)

    @pl.when(jnp.max(jnp.where(excess, 1.0, 0.0)) > 0.0)
    def _():
        j_ref[...] = _tie_cutoff(lambda cand: count((key_ref[:, :w] == thr) & (idx < cand)), need, tq,
                                 max(1, (w - 1).bit_length()))

    sel = ((key > thr) | (eq & (idx <= j_ref[...]))) & (key > KEY_NEG_INF) & (key < KEY_POS_INF)
    return jnp.where(sel, 0.0, NEG).astype(BF16)


def _select_kernel(*refs, tq, t, tn, past, topk, idx_hd, wstep, grp):
    it = iter(refs)
    iq_ref, iw_ref, ikn_ref = next(it), next(it), next(it)
    ikc_ref = next(it) if past else None
    bc_ref = next(it) if past else None
    bn_ref, key_ref, j_ref = next(it), next(it), next(it)
    qi = pl.program_id(1)
    qpos = past + qi * tq + lax.broadcasted_iota(I32, (tq, 1), 0)

    def keys(rs, kmat, kpos, valid, k_transposed=False):
        ok = (kpos >> CHUNK_SHIFT) <= (qpos >> CHUNK_SHIFT)
        if valid is not None:
            ok = ok & valid
        sc = _index_scores(iq_ref, iw_ref, rs, kmat, idx_hd, k_transposed)
        return jnp.where(ok, _sort_key(sc), KEY_NEG_INF)

    def run(wn):
        c = lax.broadcasted_iota(I32, (1, wn), 1)
        for g in range(grp):
            rs = slice(g * tq, (g + 1) * tq)
            if past:
                key_ref[rs, :past] = keys(rs, ikc_ref[0, g].astype(BF16), lax.broadcasted_iota(I32, (1, past), 1),
                                          None, k_transposed=True)
            key_ref[rs, past:past + wn] = keys(rs, ikn_ref[g, :wn, :], past + c, (c < t) if tn != t else None)
        bias = _bias_from_keys(key_ref, j_ref, past + wn, tq=grp * tq, topk=topk)
        for g in range(grp):
            rs = slice(g * tq, (g + 1) * tq)
            if past:
                bc_ref[g] = bias[rs, :past]
            bn_ref[g, :, :wn] = bias[rs, past:]
            if wn < tn:
                bn_ref[g, :, wn:] = jnp.full((tq, tn - wn), NEG, BF16)

    ncls = tn // wstep
    if ncls <= 1:
        run(tn)
    else:
        cls = ((qi + 1) * tq + wstep - 1) // wstep - 1
        for j in range(ncls):
            pl.when(cls == j)(functools.partial(run, (j + 1) * wstep))


def _select(iq, iw, ikn, *, batch, t, tq, past=0, ikc=None, layer=0):
    tn, idx_hd = ikn.shape[1], ikn.shape[2]
    nq = t // tq
    topk = min(TOPK_MAX, (past + t) // 4)
    wstep = SEL_WSTEP if (past == 0 and tn == t and tn % SEL_WSTEP == 0 and SEL_WSTEP % tq == 0) else tn
    grp = max(g for g in (1, 2, 4) if batch % g == 0 and g * tq <= 256) if nq == 1 else 1
    row = lambda b, qi: (b * nq + qi, 0)
    in_specs = [pl.BlockSpec((grp * tq, iq.shape[1]), row), pl.BlockSpec((grp * tq, LANES), row),
                pl.BlockSpec((grp, tn, idx_hd), lambda b, qi: (b, 0, 0))]
    args = [iq, iw, ikn]
    out_specs, out_shape = [], []
    if past:
        in_specs.append(pl.BlockSpec((1, grp, idx_hd, past), lambda b, qi: (layer, b, 0, 0)))
        args.append(ikc)
        out_specs.append(pl.BlockSpec((grp, tq, past), lambda b, qi: (b, qi, 0)))
        out_shape.append(jax.ShapeDtypeStruct((batch, t, past), BF16))
    out_specs.append(pl.BlockSpec((grp, tq, tn), lambda b, qi: (b, qi, 0)))
    out_shape.append(jax.ShapeDtypeStruct((batch, t, tn), BF16))
    out = pl.pallas_call(
        functools.partial(_select_kernel, tq=tq, t=t, tn=tn, past=past, topk=topk, idx_hd=idx_hd, wstep=wstep,
                          grp=grp),
        grid=(batch // grp, nq),
        in_specs=in_specs, out_specs=out_specs, out_shape=out_shape,
        scratch_shapes=[pltpu.VMEM((grp * tq, past + tn), I32), pltpu.VMEM((grp * tq, 1), I32)],
        compiler_params=_cparams(2),
        name="select",
    )(*args)
    return (out[0], out[1]) if past else (None, out[0])


def _post_kernel(x_ref, oa_ref, ob_ref, ga_ref, wa_ref, wb_ref, wg_ref, bg_ref, wo_ref, gf_ref,
                 xo_ref, h2_ref, mg_ref):
    d = x_ref.shape[1]
    x = x_ref[...]
    h = _rms(x, ga_ref[...]).astype(BF16)
    oa, ob = oa_ref[...], ob_ref[...]
    cw = 2 * LANES
    for c in range(0, d, cw):
        gate_a = jax.nn.sigmoid(_dot(h, wg_ref[:, c:c + cw]) + bg_ref[:, c:c + cw])
        gate_b = jax.nn.sigmoid(_dot(h, wg_ref[:, d + c:d + c + cw]) + bg_ref[:, d + c:d + c + cw])
        merged = gate_a * _dot(oa, wa_ref[:, c:c + cw]) + gate_b * _dot(ob, wb_ref[:, c:c + cw])
        mg_ref[:, c:c + cw] = merged.astype(BF16)
    mg = mg_ref[...]
    for c in range(0, d, cw):
        xo_ref[:, c:c + cw] = x[:, c:c + cw] + _dot(mg, wo_ref[:, c:c + cw])
    h2_ref[...] = _rms(xo_ref[...], gf_ref[...]).astype(BF16)


def _post(x, oa, ob, ga, wa, wb, wg, bg, wo, gf, *, tm):
    n, d = x.shape
    row = lambda i: (i, 0)
    const = lambda i: (0, 0)
    full = lambda a: pl.BlockSpec(a.shape, const)
    return pl.pallas_call(
        _post_kernel,
        grid=(n // tm,),
        in_specs=[pl.BlockSpec((tm, d), row)] * 3 + [full(a) for a in (ga, wa, wb, wg, bg, wo, gf)],
        out_specs=[pl.BlockSpec((tm, d), row)] * 2,
        out_shape=[jax.ShapeDtypeStruct((n, d), F32), jax.ShapeDtypeStruct((n, d), BF16)],
        scratch_shapes=[pltpu.VMEM((tm, d), BF16)],
        compiler_params=_cparams(1),
        name="post",
    )(x, oa, ob, ga, wa, wb, wg, bg, wo, gf)


def _ffn_kernel(x_ref, h_ref, wl_ref, wgt_ref, cwl_ref, cwg_ref, cbl_ref, cbg_ref, pl_ref, pg_ref,
                wd_ref, xo_ref, nl_ref, ng_ref, acc_ref, cl_ref, cg_ref, *, tm, seqs):
    i, j = pl.program_id(1), pl.program_id(2)
    nj = pl.num_programs(2)
    h = h_ref[...]
    rid = lax.broadcasted_iota(I32, (tm, 1), 0)

    def conv(w_ref, cw_ref, cb_ref, prev_ref, carry_ref, new_ref):
        u = _dot(h, w_ref[...])
        u1, u2 = pltpu.roll(u, 1, 0), pltpu.roll(u, 2, 0)
        cw = cw_ref[...]
        if seqs == 1:
            hal = jnp.where(i == 0, prev_ref[0], carry_ref[j, 6:8, :])
            u1 = jnp.where(rid == 0, hal[1:2], u1)
            u2 = jnp.where(rid == 0, hal[0:1], jnp.where(rid == 1, hal[1:2], u2))
            carry_ref[j] = u[tm - 8:, :]
            new_ref[0] = u[tm - 2:, :]
        else:
            ts = tm // seqs
            for s in range(seqs):
                hal = prev_ref[s]
                u1 = jnp.where(rid == s * ts, hal[1:2], u1)
                u2 = jnp.where(rid == s * ts, hal[0:1], jnp.where(rid == s * ts + 1, hal[1:2], u2))
                new_ref[s] = u[(s + 1) * ts - 2:(s + 1) * ts, :]
        return cb_ref[...] + u2 * cw[0:1] + u1 * cw[1:2] + u * cw[2:3]

    c_lin = conv(wl_ref, cwl_ref, cbl_ref, pl_ref, cl_ref, nl_ref)
    c_gate = conv(wgt_ref, cwg_ref, cbg_ref, pg_ref, cg_ref, ng_ref)
    act = (jax.nn.silu(c_gate) * c_lin).astype(BF16)
    part = _dot(act, wd_ref[...])

    @pl.when(j == 0)
    def _():
        acc_ref[...] = part

    @pl.when(j > 0)
    def _():
        acc_ref[...] += part

    @pl.when(j == nj - 1)
    def _():
        xo_ref[...] = x_ref[...] + acc_ref[...]


def _ffn(x, h2, w_up, conv_w, conv_b, conv_prev, w_down, *, batch, t, tm, tn, seqs=1):
    n, d = x.shape
    dff = w_down.shape[0]
    nt, nj = t // tm, dff // tn
    assert seqs == 1 or (nt == 1 and batch % seqs == 0)
    nb, tm = batch // seqs, tm * seqs
    row = lambda b, i, j: (b * nt + i, 0)
    lin = lambda b, i, j: (0, j)
    gate = lambda b, i, j: (0, nj + j)
    tile_rows = lambda b, i, j: (b * nt + i, 0, j)
    return pl.pallas_call(
        functools.partial(_ffn_kernel, tm=tm, seqs=seqs),
        grid=(nb, nt, nj),
        in_specs=[pl.BlockSpec((tm, d), row), pl.BlockSpec((tm, d), row),
                  pl.BlockSpec((d, tn), lin), pl.BlockSpec((d, tn), gate),
                  pl.BlockSpec((CONV_W, tn), lin), pl.BlockSpec((CONV_W, tn), gate),
                  pl.BlockSpec((1, tn), lin), pl.BlockSpec((1, tn), gate),
                  pl.BlockSpec((seqs, CONV_W - 1, tn), lambda b, i, j: (b, 0, j)),
                  pl.BlockSpec((seqs, CONV_W - 1, tn), lambda b, i, j: (b, 0, nj + j)),
                  pl.BlockSpec((tn, d), lambda b, i, j: (j, 0))],
        out_specs=[pl.BlockSpec((tm, d), row),
                   pl.BlockSpec((seqs, CONV_W - 1, tn), tile_rows),
                   pl.BlockSpec((seqs, CONV_W - 1, tn), tile_rows)],
        out_shape=[jax.ShapeDtypeStruct((n, d), F32),
                   jax.ShapeDtypeStruct((batch * nt, CONV_W - 1, dff), F32),
                   jax.ShapeDtypeStruct((batch * nt, CONV_W - 1, dff), F32)],
        scratch_shapes=[pltpu.VMEM((tm, d), F32), pltpu.VMEM((nj, 8, tn), F32),
                        pltpu.VMEM((nj, 8, tn), F32)],
        compiler_params=_cparams(3),
        name="ffn",
    )(x, h2, w_up, w_up, conv_w, conv_w, conv_b, conv_b, conv_prev, conv_prev, w_down)


def _final_kernel(x_ref, g_ref, o_ref):
    o_ref[...] = _rms(x_ref[...], g_ref[...])


def _final_norm(x, g, *, tm):
    n, d = x.shape
    return pl.pallas_call(
        _final_kernel, grid=(n // tm,),
        in_specs=[pl.BlockSpec((tm, d), lambda i: (i, 0)), pl.BlockSpec((1, d), lambda i: (0, 0))],
        out_specs=pl.BlockSpec((tm, d), lambda i: (i, 0)),
        out_shape=jax.ShapeDtypeStruct((n, d), F32),
        compiler_params=_cparams(1), name="final_norm",
    )(x, g)


def _rope_tables(pos, tm):
    def ang(hd):
        half = hd // 2
        inv = ROPE_THETA ** (-(jnp.arange(half, dtype=F32) / half))
        return pos.astype(F32)[:, None] * inv[None, :]

    a64, a128 = ang(64), ang(128)
    cos64 = jnp.tile(jnp.cos(a64), (1, 4))
    sin64 = jnp.tile(jnp.sin(a64), (1, 4))
    lane = jnp.arange(LANES)[None, :]
    low64 = (lane % 64) < 32
    sa64 = jnp.where(low64, -sin64, 0.0)
    sb64 = jnp.where(low64, 0.0, sin64)
    cos128 = jnp.tile(jnp.cos(a128), (1, 2))
    sin128 = jnp.tile(jnp.sin(a128), (1, 2))
    s128 = jnp.where(lane < 64, -sin128, sin128)
    tabs = (cos64, sa64, sb64, cos128, s128)
    reps = -(-tm // pos.shape[0])
    return tuple(jnp.tile(tb, (reps, 1)) for tb in tabs)


def _ffn_chunk(dff):
    best = LANES
    for k in range(1, dff // LANES + 1):
        if dff % (k * LANES) == 0 and k * LANES <= 1536:
            best = k * LANES
    return best


def kernel(x_prompt, x_sample, cache_diff_k, cache_diff_v, cache_dsa_k, cache_dsa_v, cache_idx_k,
           state_ffn_conv, norm_attn, w_in, diff_lambda, diff_subln, w_branch_a, w_branch_b, w_gate,
           b_gate, w_out, norm_ffn, w_up, conv_w, conv_b, w_down, norm_final):
    depth, dec_batch, past = cache_diff_k.shape[:3]
    batch, seq, d_model = x_prompt.shape
    dec_seq = x_sample.shape[1]
    diff_heads2, diff_hd = cache_diff_k.shape[3:]
    dsa_heads, dsa_hd = cache_dsa_k.shape[3:]
    idx_hd = cache_idx_k.shape[3]
    dw, sw, iqw = diff_heads2 * diff_hd, dsa_heads * dsa_hd, IDX_HEADS * idx_hd
    dff = w_down.shape[1]
    assert diff_hd == 64 and idx_hd == 64 and dsa_hd == LANES and 2 * diff_hd == LANES
    assert w_in.shape[2] == 3 * dw + 3 * sw + iqw + idx_hd + IDX_HEADS
    assert seq % CHUNK == 0 and dec_seq % CHUNK == 0 and past % CHUNK == 0

    o = [0]
    for size in (dw, dw, dw, sw, sw, sw, iqw, idx_hd, IDX_HEADS):
        o.append(o[-1] + size)
    wb = w_in.astype(BF16)
    w64 = jnp.concatenate([wb[:, :, o[0]:o[2]], wb[:, :, o[6]:o[7]]], axis=2)
    w128 = wb[:, :, o[3]:o[5]]
    wv = jnp.concatenate([wb[:, :, o[2]:o[3]], wb[:, :, o[5]:o[6]]], axis=2)
    wikw = jnp.pad(wb[:, :, o[7]:o[9]], ((0, 0), (0, 0), (0, LANES - idx_hd - IDX_HEADS)))
    wa_b, wb_b, wg_b, wo_b = (w.astype(BF16) for w in (w_branch_a, w_branch_b, w_gate, w_out))
    wup_b, wdn_b = w_up.astype(BF16), w_down.astype(BF16)
    r2 = lambda a: a.reshape(a.shape[0], 1, a.shape[1])

    cfgs = []
    for name, x, bsz, t, pst in (("p", x_prompt, batch, seq, 0), ("s", x_sample, dec_batch, dec_seq, past)):
        n = bsz * t
        tm = _pick(n, 256)
        tq = _pick(t, 512)
        tkn = max(tq, LANES)
        pos = pst + jnp.arange(t, dtype=jnp.int32)
        cfgs.append(dict(name=name, x=x.reshape(n, d_model), bsz=bsz, t=t, past=pst, n=n, tm=tm, tq=tq,
                         tkn=tkn, tn_rows=max(t, tkn), tabs=_rope_tables(pos, tm), n_tab=max(t // tm, 1),
                         tqs=_pick(t, 512), tmf=_pick(t, 512), rows=[]))

    caches = None
    if past:
        flat = lambda a: a.reshape(depth, dec_batch, past * a.shape[3], a.shape[4])
        caches = dict(dk=jnp.transpose(cache_diff_k, (0, 1, 3, 4, 2)), dv=flat(cache_diff_v),
                      sk=flat(cache_dsa_k), sv=flat(cache_dsa_v),
                      ik=jnp.transpose(cache_idx_k, (0, 1, 3, 2)))
    tkc = _pick(past, 1024) if past else 0
    tn_ffn = _ffn_chunk(dff)

    for l in range(depth):
        lam_init = 0.8 - 0.6 * math.exp(-0.3 * l)
        for cf in cfgs:
            bsz, t, pst, n = cf["bsz"], cf["t"], cf["past"], cf["n"]
            stacked = t % cf["tm"] == 0 and cf["tm"] % LANES == 0
            (dq, dkb, dkf, iq, sq, skb, skf, dvb, dvf, svb, svf, ikb, ikf, iw) = _proj(
                cf["x"], r2(norm_attn)[l], (w64[l], w128[l], wv[l], wikw[l]), cf["tabs"],
                tm=cf["tm"], n_tab=cf["n_tab"], dw=dw, sw=sw, iqw=iqw, idx_hd=idx_hd,
                stack=(depth, l, bsz, t, cf.get("stk")) if stacked else None)
            if stacked:
                cf["stk"] = (dkf, skf, dvf, svf, ikf)

            def padrows(a):
                if cf["tn_rows"] == t:
                    return a
                a = a.reshape(bsz, t, a.shape[1])
                a = jnp.pad(a, ((0, 0), (0, cf["tn_rows"] - t), (0, 0)))
                return a.reshape(bsz * cf["tn_rows"], -1)

            has_cache = bool(pst)
            common = dict(batch=bsz, t=t, tq=cf["tq"], tkn=cf["tkn"], past=pst, layer=l, tkc=tkc)
            oa = _attn(dq, padrows(dkb), padrows(dvb), diff=True,
                       cache=(caches["dk"], caches["dv"]) if has_cache else None,
                       lam=diff_lambda[l], subln=r2(diff_subln)[l], lam_init=lam_init, **common)
            ikn = padrows(ikb).reshape(bsz, cf["tn_rows"], idx_hd)
            bias_c, bias_n = _select(iq, iw, ikn, batch=bsz, t=t, tq=cf["tqs"], past=pst,
                                     ikc=caches["ik"] if has_cache else None, layer=l)
            ob = _attn(sq, padrows(skb), padrows(svb), diff=False,
                       cache=(caches["sk"], caches["sv"]) if has_cache else None,
                       bias_c=bias_c, bias_n=bias_n, **common)
            x_mid, h2 = _post(cf["x"], oa, ob, r2(norm_attn)[l], wa_b[l], wb_b[l], wg_b[l], r2(b_gate)[l],
                              wo_b[l], r2(norm_ffn)[l], tm=cf["tm"])
            prev = state_ffn_conv[l] if has_cache else jnp.zeros((bsz, CONV_W - 1, 2 * dff), F32)
            seqs = 1
            if cf["tmf"] == t:
                seqs = max(g for g in (1, 2, 4, 8) if bsz % g == 0 and g * t <= 256)
            x_new, new_l, new_g = _ffn(x_mid, h2, wup_b[l], conv_w[l], r2(conv_b)[l], prev, wdn_b[l],
                                       batch=bsz, t=t, tm=cf["tmf"], tn=tn_ffn, seqs=seqs)
            last_tile = lambda a: a.reshape(bsz, t // cf["tmf"], CONV_W - 1, dff)[:, -1]
            cf["x"] = x_new
            conv_state = jnp.concatenate([last_tile(new_l), last_tile(new_g)], axis=-1)
            if stacked:
                cf["rows"].append((conv_state,))
            else:
                cf["rows"].append((dkf.reshape(bsz, t, diff_heads2, diff_hd),
                                   dvf.reshape(bsz, t, diff_heads2 // 2, 2 * diff_hd),
                                   skf.reshape(bsz, t, dsa_heads, dsa_hd),
                                   svf.reshape(bsz, t, dsa_heads, dsa_hd),
                                   ikf.reshape(bsz, t, idx_hd), conv_state))

    outs, stacks = [], []
    for cf in cfgs:
        y = _final_norm(cf["x"], norm_final.reshape(1, -1), tm=cf["tm"])
        outs.append(y.reshape(cf["bsz"], cf["t"], d_model))
        per_layer = [jnp.stack(z, axis=0) for z in zip(*cf["rows"])]
        if "stk" in cf:
            dkf, skf, dvf, svf, ikf = cf["stk"]
            bsz, t = cf["bsz"], cf["t"]
            per_layer = [jnp.transpose(dkf, (0, 1, 4, 2, 3)),
                         dvf.reshape(depth, bsz, t, diff_heads2 // 2, 2 * diff_hd),
                         skf.reshape(depth, bsz, t, dsa_heads, dsa_hd),
                         svf.reshape(depth, bsz, t, dsa_heads, dsa_hd),
                         jnp.transpose(ikf, (0, 1, 3, 2))] + per_layer
        stacks.append(per_layer)
    return (outs[0], outs[1], *stacks[0], *stacks[1])
```
